```python
import math
import jax, jax.numpy as jnp
from jax import lax
import numpy as np

D_MODEL = 1024
BATCH = 2
SEQ = 16384
DEPTH = 2

A_GROUPS = 4
A_GROUP_CH = 128
A_CHUNK = 128
A_WIDTH = A_GROUPS * A_GROUP_CH
B_HEADS = 4
B_HEAD_DIM = 128
B_CONV = 4
B_CHUNK = 64
B_WIDTH = B_HEADS * B_HEAD_DIM
C_PATTERNS = ((128, 1), (512, 4), (2048, 16))
C_GROUPS = 3
C_HEADS = 4
C_HEAD_DIM = 128
C_BLOCK = 128
C_WIDTH = C_HEADS * C_HEAD_DIM
C_QK_W = C_GROUPS * C_WIDTH
REL_BUCKETS = 32
REL_MAX_DIST = 2048
N_BRANCH = 3
BRANCH_W = 512
N_IN = 2 * A_WIDTH + 4 * B_WIDTH + 2 * B_HEADS + 2 * C_QK_W + C_WIDTH + N_BRANCH * D_MODEL
MOE_GROUPS = 4
MOE_EXPERTS_PER_GROUP = 8
MOE_EXPERTS = MOE_GROUPS * MOE_EXPERTS_PER_GROUP
MOE_TOPK = 2
MOE_FF = 512
MOE_BLOCK = 128
EPS = 1e-6

kernel_name = 'hybrid_gated_parallel_mixers_hier_moe'


def rmsnorm(x, g):
    xf = x.astype(jnp.float32)
    y = xf * lax.rsqrt(jnp.mean(xf * xf, axis=-1, keepdims=True) + EPS)
    return y.astype(x.dtype) * g


def layernorm(x, g, b):
    xf = x.astype(jnp.float32)
    mu = jnp.mean(xf, axis=-1, keepdims=True)
    xc = xf - mu
    y = xc * lax.rsqrt(jnp.mean(xc * xc, axis=-1, keepdims=True) + EPS)
    return y.astype(x.dtype) * g + b


def l2norm(x):
    return x * lax.rsqrt(jnp.sum(x * x, axis=-1, keepdims=True) + EPS)


def spatial_gating_mixer(u, v, w_s, b_s, ln_g, ln_b):
    B, T, _ = u.shape
    u = jax.nn.gelu(u)
    v = layernorm(jax.nn.gelu(v), ln_g, ln_b)
    v = v.reshape(B, T // A_CHUNK, A_CHUNK, A_GROUPS, A_GROUP_CH)
    w = jnp.where(np.tril(np.ones((A_CHUNK, A_CHUNK), dtype=bool)), w_s, 0.0)
    sv = jnp.einsum('gts,bnsgc->bntgc', w, v) + jnp.transpose(b_s)[:, :, None]
    return u * sv.reshape(B, T, A_WIDTH)


def causal_depthwise_conv(x, w):
    K, C = w.shape
    return lax.conv_general_dilated(x, w[:, None, :].astype(x.dtype), window_strides=(1,),
                                    padding=[(K - 1, 0)], dimension_numbers=('NWC', 'WIO', 'NWC'),
                                    feature_group_count=C)


def chunked_gated_delta_rule(q, k, v, g, beta):
    B, T, H, dk = q.shape
    dv = v.shape[-1]
    C = B_CHUNK
    NC = T // C
    f32 = jnp.float32
    q = l2norm(q.astype(f32)) * (dk ** -0.5)
    k = l2norm(k.astype(f32))
    v = v.astype(f32)

    def chunk(a):
        return a.reshape(B, NC, C, H, a.shape[-1]).transpose(0, 3, 1, 2, 4)

    q, k, v = chunk(q), chunk(k), chunk(v)
    g = g.astype(f32).reshape(B, NC, C, H).transpose(0, 3, 1, 2)
    beta = beta.astype(f32).reshape(B, NC, C, H).transpose(0, 3, 1, 2)
    gc = jnp.cumsum(g, axis=-1)
    incl = np.tril(np.ones((C, C), dtype=bool))
    strict = np.tril(np.ones((C, C), dtype=bool), -1)
    decay = jnp.exp(jnp.where(incl, gc[..., :, None] - gc[..., None, :], -jnp.inf))
    kb = k * beta[..., None]
    vb = v * beta[..., None]
    lower = jnp.where(strict, jnp.einsum('bhnid,bhnjd->bhnij', kb, k) * decay, 0.0)
    a_mat = lower + jnp.eye(C, dtype=f32)
    rhs = jnp.concatenate([vb, kb * jnp.exp(gc)[..., None]], axis=-1)
    sol = lax.linalg.triangular_solve(a_mat, rhs, left_side=True, lower=True, unit_diagonal=True)
    u, w = sol[..., :dv], sol[..., dv:]
    attn = jnp.where(incl, jnp.einsum('bhnid,bhnjd->bhnij', q, k) * decay, 0.0)
    q_dec = q * jnp.exp(gc)[..., None]
    g_last = gc[..., -1]
    k_dec = k * jnp.exp(g_last[..., None] - gc)[..., None]

    def step(S, xs):
        q_i, attn_i, u_i, w_i, k_i, gl_i = xs
        v_new = u_i - jnp.einsum('bhcd,bhde->bhce', w_i, S)
        o = jnp.einsum('bhcd,bhde->bhce', q_i, S) + jnp.einsum('bhij,bhje->bhie', attn_i, v_new)
        S = S * jnp.exp(gl_i)[..., None, None] + jnp.einsum('bhcd,bhce->bhde', k_i, v_new)
        return S, o

    xs = (jnp.moveaxis(q_dec, 2, 0), jnp.moveaxis(attn, 2, 0), jnp.moveaxis(u, 2, 0),
          jnp.moveaxis(w, 2, 0), jnp.moveaxis(k_dec, 2, 0), jnp.moveaxis(g_last, -1, 0))
    S0 = jnp.zeros((B, H, dk, dv), f32)
    _, o = lax.scan(step, S0, xs)
    return o.transpose(1, 0, 3, 2, 4).reshape(B, T, H, dv)


def gated_deltanet_mixer(q, k, v, z, beta_logit, alpha_logit, conv_w, a_log, dt_bias, norm_g):
    B, T, _ = q.shape
    qkv = jax.nn.silu(causal_depthwise_conv(jnp.concatenate([q, k, v], axis=-1), conv_w))
    q, k, v = jnp.split(qkv, 3, axis=-1)
    shp = (B, T, B_HEADS, B_HEAD_DIM)
    beta = jax.nn.sigmoid(beta_logit.astype(jnp.float32))
    g = -jnp.exp(a_log.astype(jnp.float32)) * jax.nn.softplus(alpha_logit.astype(jnp.float32) + dt_bias.astype(jnp.float32))
    o = chunked_gated_delta_rule(q.reshape(shp), k.reshape(shp), v.reshape(shp), g, beta).astype(q.dtype)
    o = rmsnorm(o, norm_g) * jax.nn.silu(z.reshape(shp))
    return o.reshape(B, T, B_WIDTH)


def _t5_bucket(dist):
    max_exact = REL_BUCKETS // 2
    n = np.maximum(dist, 0)
    large = max_exact + (np.log(np.maximum(n, 1) / max_exact) / math.log(REL_MAX_DIST / max_exact)
                         * (REL_BUCKETS - max_exact)).astype(np.int32)
    large = np.minimum(large, REL_BUCKETS - 1)
    return np.where(n < max_exact, n, large).astype(np.int32)


def _band_pattern(window, dil):
    i = np.arange(C_BLOCK)[:, None]
    kk = np.arange(2 * C_BLOCK)[None, :]
    j = C_BLOCK + i - kk
    band = (j >= 0) & (j <= window // dil)
    bucket = _t5_bucket(np.clip(j, 0, None) * dil)
    return band, bucket


def dilated_window_attention(q, k, v, bias, band, dil):
    B, T, H, dh = q.shape
    L = T // dil
    nblk = -(-L // C_BLOCK)
    Lp = nblk * C_BLOCK

    def to_blocks(a):
        a = jnp.swapaxes(a.reshape(B, L, dil, H, dh), 1, 2)
        a = jnp.pad(a, ((0, 0), (0, 0), (0, Lp - L), (0, 0), (0, 0)))
        return a.reshape(B, dil, nblk, C_BLOCK, H, dh)

    def with_prev(a):
        prev = jnp.pad(a, ((0, 0), (0, 0), (1, 0), (0, 0), (0, 0), (0, 0)))[:, :, :-1]
        return jnp.concatenate([prev, a], axis=3)

    qb = to_blocks(q)
    kb = with_prev(to_blocks(k))
    vb = with_prev(to_blocks(v))
    logits = jnp.einsum('brnqhd,brnkhd->brnhqk', qb, kb, preferred_element_type=jnp.float32) * (dh ** -0.5) + bias
    first = np.arange(2 * C_BLOCK) >= C_BLOCK
    valid = band[None] & ((np.arange(nblk) > 0)[:, None, None] | first[None, None, :])
    logits = jnp.where(valid[:, None], logits, -jnp.inf)
    mx = jnp.max(logits, axis=-1)
    p = jnp.exp(logits - mx[..., None])
    den = jnp.sum(p, axis=-1)
    num = jnp.einsum('brnhqk,brnkhd->brnqhd', p, vb.astype(jnp.float32))

    def from_blocks(a):
        a = a.reshape((B, dil, Lp) + a.shape[4:])[:, :, :L]
        return jnp.swapaxes(a, 1, 2).reshape((B, T) + a.shape[3:])

    return from_blocks(num), from_blocks(jnp.swapaxes(den, 3, 4)), from_blocks(jnp.swapaxes(mx, 3, 4))


def dilated_attention_mixer(q, k, v, rel_bias):
    B, T, _ = q.shape
    q = q.reshape(B, T, C_GROUPS, C_HEADS, C_HEAD_DIM)
    k = k.reshape(B, T, C_GROUPS, C_HEADS, C_HEAD_DIM)
    v = v.reshape(B, T, C_HEADS, C_HEAD_DIM)
    nums, dens, maxs = [], [], []
    for gi, (window, dil) in enumerate(C_PATTERNS):
        band, bucket = _band_pattern(window, dil)
        bias = jnp.transpose(rel_bias[bucket][..., gi * C_HEADS:(gi + 1) * C_HEADS], (2, 0, 1)).astype(jnp.float32)
        num, den, mx = dilated_window_attention(q[:, :, gi], k[:, :, gi], v, bias, band, dil)
        nums.append(num)
        dens.append(den)
        maxs.append(mx)
    maxs = jnp.stack(maxs)
    scale = jnp.exp(maxs - jnp.max(maxs, axis=0, keepdims=True))
    num = jnp.sum(scale[..., None] * jnp.stack(nums), axis=0)
    den = jnp.sum(scale * jnp.stack(dens), axis=0)
    out = num / den[..., None]
    return out.reshape(B, T, C_WIDTH).astype(v.dtype)


def hierarchical_moe(h, w_group, w_expert, w1, w3, w2):
    B, T, D = h.shape
    N = B * T
    xt = h.reshape(N, D)
    grp_prob = jax.nn.softmax(jnp.einsum('nd,dg->ng', xt, w_group).astype(jnp.float32), axis=-1)
    grp_p, grp_idx = lax.top_k(grp_prob, 1)
    exp_logits = jnp.einsum('nd,de->ne', xt, w_expert).astype(jnp.float32).reshape(N, MOE_GROUPS, MOE_EXPERTS_PER_GROUP)
    sel_logits = jnp.take_along_axis(exp_logits, grp_idx[:, :, None], axis=1)[:, 0]
    sel_prob = jax.nn.softmax(sel_logits, axis=-1)
    top_p, top_i = lax.top_k(sel_prob, MOE_TOPK)
    gate = grp_p * (top_p / jnp.sum(top_p, axis=-1, keepdims=True))
    expert = grp_idx * MOE_EXPERTS_PER_GROUP + top_i

    A = N * MOE_TOPK
    e = expert.reshape(A)
    tok = jnp.repeat(jnp.arange(N, dtype=jnp.int32), MOE_TOPK)
    gw = gate.reshape(A)
    order = jnp.argsort(e)
    e_s, tok_s, gw_s = e[order], tok[order], gw[order]
    counts = jnp.bincount(e, length=MOE_EXPERTS)
    padded = (counts + MOE_BLOCK - 1) // MOE_BLOCK * MOE_BLOCK
    pad_end = jnp.cumsum(padded)
    pad_start = pad_end - padded
    start = jnp.cumsum(counts) - counts
    dest = pad_start[e_s] + jnp.arange(A) - start[e_s]
    R = A + MOE_EXPERTS * MOE_BLOCK
    row_tok = jnp.zeros((R,), jnp.int32).at[dest].set(tok_s)
    row_gate = jnp.zeros((R,), jnp.float32).at[dest].set(gw_s)
    n_blocks = R // MOE_BLOCK
    blk_expert = jnp.minimum(jnp.searchsorted(pad_end, jnp.arange(n_blocks) * MOE_BLOCK, side='right'), MOE_EXPERTS - 1)
    xb = xt[row_tok].reshape(n_blocks, MOE_BLOCK, D)

    def expert_block(args):
        xblk, eid = args
        hid = jax.nn.silu(xblk @ w1[eid]) * (xblk @ w3[eid])
        return hid @ w2[eid]

    yb = lax.map(expert_block, (xb, blk_expert)).reshape(R, D)
    y = jnp.zeros((N, D), h.dtype).at[row_tok].add(yb * row_gate[:, None].astype(yb.dtype))
    return y.reshape(B, T, D)


def setup_inputs(seed: int = 0) -> dict:
    key = jax.random.key(seed)
    ks = jax.random.split(key, 24)
    f32 = jnp.float32

    def nrm(k, shape, scale):
        return jax.random.normal(k, shape, f32) * scale

    x = nrm(ks[0], (BATCH, SEQ, D_MODEL), 1.0)
    norm_mix_g = 1.0 + nrm(ks[1], (DEPTH, D_MODEL), 0.02)
    w_in = nrm(ks[2], (DEPTH, D_MODEL, N_IN), D_MODEL ** -0.5)
    a_ln_g = 1.0 + nrm(ks[3], (DEPTH, A_WIDTH), 0.02)
    a_ln_b = nrm(ks[4], (DEPTH, A_WIDTH), 0.02)
    a_w_s = nrm(ks[5], (DEPTH, A_GROUPS, A_CHUNK, A_CHUNK), A_CHUNK ** -0.5)
    a_b_s = 1.0 + nrm(ks[6], (DEPTH, A_GROUPS, A_CHUNK), 0.1)
    b_conv_w = nrm(ks[7], (DEPTH, B_CONV, 3 * B_WIDTH), B_CONV ** -0.5)
    b_a_log = jnp.log(jax.random.uniform(ks[8], (DEPTH, B_HEADS), f32, 1.0, 16.0))
    dt = jnp.exp(jax.random.uniform(ks[9], (DEPTH, B_HEADS), f32, math.log(1e-3), math.log(1e-1)))
    b_dt_bias = dt + jnp.log(-jnp.expm1(-dt))
    b_norm_g = 1.0 + nrm(ks[10], (DEPTH, B_HEAD_DIM), 0.02)
    rel_bias = nrm(ks[11], (REL_BUCKETS, C_GROUPS * C_HEADS), 0.5)
    w_branch = nrm(ks[12], (DEPTH, N_BRANCH, BRANCH_W, D_MODEL), BRANCH_W ** -0.5)
    w_out = nrm(ks[13], (DEPTH, D_MODEL, D_MODEL), D_MODEL ** -0.5)
    norm_ffn_g = 1.0 + nrm(ks[14], (DEPTH, D_MODEL), 0.02)
    w_group = nrm(ks[15], (DEPTH, D_MODEL, MOE_GROUPS), D_MODEL ** -0.5)
    w_expert = nrm(ks[16], (DEPTH, D_MODEL, MOE_EXPERTS), D_MODEL ** -0.5)
    w1 = nrm(ks[17], (DEPTH, MOE_EXPERTS, D_MODEL, MOE_FF), D_MODEL ** -0.5)
    w3 = nrm(ks[18], (DEPTH, MOE_EXPERTS, D_MODEL, MOE_FF), D_MODEL ** -0.5)
    w2 = nrm(ks[19], (DEPTH, MOE_EXPERTS, MOE_FF, D_MODEL), MOE_FF ** -0.5)
    final_norm_g = 1.0 + nrm(ks[20], (D_MODEL,), 0.02)
    return {'x': x, 'norm_mix_g': norm_mix_g, 'w_in': w_in, 'a_ln_g': a_ln_g, 'a_ln_b': a_ln_b,
            'a_w_s': a_w_s, 'a_b_s': a_b_s, 'b_conv_w': b_conv_w, 'b_a_log': b_a_log,
            'b_dt_bias': b_dt_bias, 'b_norm_g': b_norm_g, 'rel_bias': rel_bias, 'w_branch': w_branch,
            'w_out': w_out, 'norm_ffn_g': norm_ffn_g, 'w_group': w_group, 'w_expert': w_expert,
            'w1': w1, 'w3': w3, 'w2': w2, 'final_norm_g': final_norm_g}


def reference(x, norm_mix_g, w_in, a_ln_g, a_ln_b, a_w_s, a_b_s, b_conv_w, b_a_log, b_dt_bias,
              b_norm_g, rel_bias, w_branch, w_out, norm_ffn_g, w_group, w_expert, w1, w3, w2,
              final_norm_g):
    B, T, D = x.shape
    sizes = [A_WIDTH, A_WIDTH, B_WIDTH, B_WIDTH, B_WIDTH, B_WIDTH, B_HEADS, B_HEADS,
             C_QK_W, C_QK_W, C_WIDTH, N_BRANCH * D_MODEL]
    splits = [int(s) for s in np.cumsum(sizes)[:-1]]
    for l in range(DEPTH):
        h = rmsnorm(x, norm_mix_g[l])
        proj = jnp.einsum('btd,dn->btn', h, w_in[l])
        (a_u, a_v, b_q, b_k, b_v, b_z, b_beta, b_alpha, c_q, c_k, c_v, gates) = jnp.split(proj, splits, axis=-1)
        y_a = spatial_gating_mixer(a_u, a_v, a_w_s[l], a_b_s[l], a_ln_g[l], a_ln_b[l])
        y_b = gated_deltanet_mixer(b_q, b_k, b_v, b_z, b_beta, b_alpha, b_conv_w[l], b_a_log[l],
                                   b_dt_bias[l], b_norm_g[l])
        y_c = dilated_attention_mixer(c_q, c_k, c_v, rel_bias)
        gates = jax.nn.sigmoid(gates.reshape(B, T, N_BRANCH, D))
        merged = (gates[:, :, 0] * (y_a @ w_branch[l, 0])
                  + gates[:, :, 1] * (y_b @ w_branch[l, 1])
                  + gates[:, :, 2] * (y_c @ w_branch[l, 2]))
        x = x + merged @ w_out[l]
        h = rmsnorm(x, norm_ffn_g[l])
        x = x + hierarchical_moe(h, w_group[l], w_expert[l], w1[l], w3[l], w2[l])
    return rmsnorm(x, final_norm_g)
```

```python
import functools
import math

import numpy as np
import jax
import jax.numpy as jnp
from jax import lax
from jax.experimental import pallas as pl
from jax.experimental.pallas import tpu as pltpu

F32 = jnp.float32
BF16 = jnp.bfloat16

D_MODEL = 1024
A_GROUPS, A_CHUNK, A_WIDTH = 4, 128, 512
B_HEADS, B_HEAD_DIM, B_CONV, B_WIDTH = 4, 128, 4, 512
C_PATTERNS = ((128, 1), (512, 4), (2048, 16))
C_HEADS, C_HEAD_DIM, C_BLOCK, C_WIDTH = 4, 128, 128, 512
REL_BUCKETS, REL_MAX_DIST = 32, 2048
MOE_GROUPS, MOE_PER_GROUP, MOE_EXPERTS, MOE_FF = 4, 8, 32, 512
EPS = 1e-6

LANES = 128
SUBLANES = 8
VMEM_LIMIT = 48 * 1024 * 1024

PROJ_W = 9728
GATE_OFF, AU_OFF, AV_OFF = 0, 3072, 3584
BQ_OFF, BK_OFF, BV_OFF, BZ_OFF = 4096, 4608, 5120, 5632
CQ_OFF, CK_OFF, CV_OFF = 6144, 7680, 9216
WBLK = 512
PROJ_WBLKS = PROJ_W // WBLK

TM_PROJ, TN_PROJ = 1024, 512
TM_GMLP = 512
TB_DELTA = 512
CHUNK = 128
QB_ATTN = 512
TM_MERGE = 256
TM_ROUTER = 512
TD_DISPATCH = 512
BM_EXPERT = 256
TC_COMBINE = 256


def _cparams(*sem):
    return pltpu.CompilerParams(dimension_semantics=sem, vmem_limit_bytes=VMEM_LIMIT)


def _dot(a, b):
    return jnp.dot(a, b, preferred_element_type=F32)


def _dot_nt(a, b):
    return lax.dot_general(a, b, (((1,), (1,)), ((), ())), preferred_element_type=F32)


def _sigmoid(x):
    return 1.0 / (1.0 + jnp.exp(-x))


def _silu(x):
    return x * _sigmoid(x)


def _gelu_tanh(x):
    c = math.sqrt(2.0 / math.pi)
    return x * (0.5 * (1.0 + jnp.tanh(c * (x + 0.044715 * (x * x * x)))))


def _softplus(x):
    return jnp.maximum(x, 0.0) + jnp.log(1.0 + jnp.exp(-jnp.abs(x)))


def _rms_matmul_kernel(x_ref, g_ref, w_ref, o_ref, h_ref):
    @pl.when(pl.program_id(1) == 0)
    def _():
        x = x_ref[...]
        ms = jnp.mean(x * x, axis=-1, keepdims=True)
        h_ref[...] = (x * lax.rsqrt(ms + EPS) * g_ref[...]).astype(h_ref.dtype)

    o_ref[...] = _dot(h_ref[...], w_ref[...]).astype(o_ref.dtype)


def _rms_matmul(x2d, g, w, out_dtype, tn):
    n, d = x2d.shape
    ncols = w.shape[1]
    tm = min(TM_PROJ, n)
    return pl.pallas_call(
        _rms_matmul_kernel,
        grid=(n // tm, ncols // tn),
        in_specs=[pl.BlockSpec((tm, d), lambda i, j: (i, 0)),
                  pl.BlockSpec((1, d), lambda i, j: (0, 0)),
                  pl.BlockSpec((d, tn), lambda i, j: (0, j))],
        out_specs=pl.BlockSpec((tm, tn), lambda i, j: (i, j)),
        out_shape=jax.ShapeDtypeStruct((n, ncols), out_dtype),
        scratch_shapes=[pltpu.VMEM((tm, d), BF16)],
        compiler_params=_cparams("parallel", "arbitrary"),
        name="rms_matmul",
    )(x2d, g.reshape(1, d), w)


def _gmlp_kernel(u_ref, v_ref, w_ref, b_ref, lng_ref, lnb_ref, o_ref):
    tm = u_ref.shape[0]
    u = _gelu_tanh(u_ref[...].astype(F32))
    v = _gelu_tanh(v_ref[...].astype(F32))
    mu = jnp.mean(v, axis=-1, keepdims=True)
    vc = v - mu
    var = jnp.mean(vc * vc, axis=-1, keepdims=True)
    vn = (vc * lax.rsqrt(var + EPS) * lng_ref[...] + lnb_ref[...]).astype(BF16)
    for c in range(tm // A_CHUNK):
        rows = slice(c * A_CHUNK, (c + 1) * A_CHUNK)
        for g in range(A_GROUPS):
            cols = slice(g * LANES, (g + 1) * LANES)
            sv = _dot(w_ref[g], vn[rows, cols]) + b_ref[:, cols]
            o_ref[rows, cols] = (u[rows, cols] * sv).astype(o_ref.dtype)


def _gmlp(proj, w_s, b_s, ln_g, ln_b):
    n = proj.shape[0]
    tm = min(TM_GMLP, n)
    causal = np.tril(np.ones((A_CHUNK, A_CHUNK), dtype=bool))
    w = jnp.where(causal, w_s, 0.0).astype(BF16)
    bias = jnp.repeat(jnp.transpose(b_s), LANES, axis=1).astype(F32)
    return pl.pallas_call(
        _gmlp_kernel,
        grid=(n // tm,),
        in_specs=[pl.BlockSpec((tm, WBLK), lambda i: (i, AU_OFF // WBLK)),
                  pl.BlockSpec((tm, WBLK), lambda i: (i, AV_OFF // WBLK)),
                  pl.BlockSpec((A_GROUPS, A_CHUNK, A_CHUNK), lambda i: (0, 0, 0)),
                  pl.BlockSpec((A_CHUNK, A_WIDTH), lambda i: (0, 0)),
                  pl.BlockSpec((1, A_WIDTH), lambda i: (0, 0)),
                  pl.BlockSpec((1, A_WIDTH), lambda i: (0, 0))],
        out_specs=pl.BlockSpec((tm, A_WIDTH), lambda i: (i, 0)),
        out_shape=jax.ShapeDtypeStruct((n, A_WIDTH), BF16),
        compiler_params=_cparams("parallel"),
        name="gmlp",
    )(proj, proj, w, bias, ln_g.reshape(1, -1), ln_b.reshape(1, -1))


def _deltanet_kernel(q_ref, k_ref, v_ref, z_ref, p2_ref, cw_ref, alog_ref, dtb_ref, ng_ref, o_ref,
                     s_ref, carry_ref, qkv_ref):
    tb = q_ref.shape[0]
    nchunk = tb // CHUNK

    @pl.when(pl.program_id(1) == 0)
    def _():
        s_ref[...] = jnp.zeros_like(s_ref)
        carry_ref[...] = jnp.zeros_like(carry_ref)

    row8 = lax.broadcasted_iota(jnp.int32, (SUBLANES, B_WIDTH), 0)
    for idx, ref in enumerate((q_ref, k_ref, v_ref)):
        x = ref[...].astype(F32)
        w = cw_ref[:, idx * B_WIDTH:(idx + 1) * B_WIDTH]
        prev = carry_ref[idx]
        acc = x * w[B_CONV - 1:B_CONV]
        for s in range(1, B_CONV):
            xs = pltpu.roll(x, s, axis=0)
            ps = pltpu.roll(prev, s, axis=0)
            head = jnp.where(row8 < s, ps, xs[:SUBLANES])
            xs = jnp.concatenate([head, xs[SUBLANES:]], axis=0)
            acc = acc + xs * w[B_CONV - 1 - s:B_CONV - s]
        carry_ref[idx] = x[tb - SUBLANES:]
        qkv_ref[idx] = _silu(acc)

    ri = lax.broadcasted_iota(jnp.int32, (CHUNK, CHUNK), 0)
    ci = lax.broadcasted_iota(jnp.int32, (CHUNK, CHUNK), 1)
    incl = ri >= ci
    strict = ri > ci
    tril_f = incl.astype(F32)
    alog = alog_ref[...]
    dtb = dtb_ref[...]
    ng = ng_ref[...]

    def chunk_body(c, carry):
        r0 = pl.multiple_of(c * CHUNK, CHUNK)
        rows = pl.ds(r0, CHUNK)
        p2 = p2_ref[rows, :]
        g_all = -jnp.exp(alog) * _softplus(p2 + dtb)
        gc_all = jnp.dot(tril_f, g_all, preferred_element_type=F32, precision=lax.Precision.HIGHEST)
        gc_t = gc_all.T
        beta_all = _sigmoid(p2)
        for h in range(B_HEADS):
            cols = slice(h * B_HEAD_DIM, (h + 1) * B_HEAD_DIM)
            q = qkv_ref[0, rows, cols]
            k = qkv_ref[1, rows, cols]
            v = qkv_ref[2, rows, cols]
            q = q * (lax.rsqrt(jnp.sum(q * q, axis=-1, keepdims=True) + EPS) * (B_HEAD_DIM ** -0.5))
            k = k * lax.rsqrt(jnp.sum(k * k, axis=-1, keepdims=True) + EPS)
            beta = beta_all[:, h:h + 1]
            gc_col = jnp.broadcast_to(gc_all[:, B_HEADS + h:B_HEADS + h + 1], (CHUNK, B_HEAD_DIM))
            gc_row = gc_t[B_HEADS + h:B_HEADS + h + 1, :]
            decay = jnp.exp(jnp.where(incl, gc_col - gc_row, -jnp.inf))
            kb = k * beta
            vb = v * beta
            egc = jnp.exp(gc_col)
            kbf = k.astype(BF16)
            aq = _dot_nt(jnp.concatenate([kb, q], axis=0).astype(BF16), kbf)
            low = jnp.where(strict, aq[:CHUNK] * decay, 0.0)
            attn = aq[CHUNK:] * decay
            m = -low
            mb = m.astype(BF16)
            p = _dot(mb, mb)
            r = m
            for _ in range(5):
                rp = _dot(jnp.concatenate([r, p], axis=0).astype(BF16), p.astype(BF16))
                r = r + p + rp[:CHUNK]
                p = rp[CHUNK:]
            r = r + p + _dot(r.astype(BF16), p.astype(BF16))
            rhs = jnp.concatenate([vb, kb * egc], axis=1)
            sol = rhs + _dot(r.astype(BF16), rhs.astype(BF16))
            u = sol[:, :B_HEAD_DIM]
            w = sol[:, B_HEAD_DIM:]
            q_dec = q * egc
            g_last = gc_col[CHUNK - 1:CHUNK, :]
            k_dec = k * jnp.exp(g_last - gc_col)
            s_old = s_ref[h]
            wq = _dot(jnp.concatenate([w, q_dec], axis=0).astype(BF16), s_old.astype(BF16))
            v_new = u - wq[:CHUNK]
            vnb = v_new.astype(BF16)
            o = wq[CHUNK:] + _dot(attn.astype(BF16), vnb)
            s_ref[h] = s_old * jnp.exp(g_last) + _dot(k_dec.T.astype(BF16), vnb)
            z = z_ref[rows, cols].astype(F32)
            on = o * lax.rsqrt(jnp.mean(o * o, axis=-1, keepdims=True) + EPS) * ng
            o_ref[rows, cols] = (on * _silu(z)).astype(o_ref.dtype)
        return carry

    lax.fori_loop(0, nchunk, chunk_body, 0)


def _deltanet(proj, p2, conv_w, a_log, dt_bias, norm_g, batch, seq):
    tb = min(TB_DELTA, seq)
    nt = seq // tb

    def col(off):
        return pl.BlockSpec((tb, WBLK), lambda b, t: (b * nt + t, off // WBLK))

    pad = LANES - 2 * B_HEADS
    alog = jnp.pad(a_log.astype(F32), (B_HEADS, pad)).reshape(1, LANES)
    dtb = jnp.pad(dt_bias.astype(F32), (B_HEADS, pad)).reshape(1, LANES)
    return pl.pallas_call(
        _deltanet_kernel,
        grid=(batch, nt),
        in_specs=[col(BQ_OFF), col(BK_OFF), col(BV_OFF), col(BZ_OFF),
                  pl.BlockSpec((tb, LANES), lambda b, t: (b * nt + t, 0)),
                  pl.BlockSpec((B_CONV, 3 * B_WIDTH), lambda b, t: (0, 0)),
                  pl.BlockSpec((1, LANES), lambda b, t: (0, 0)),
                  pl.BlockSpec((1, LANES), lambda b, t: (0, 0)),
                  pl.BlockSpec((1, B_HEAD_DIM), lambda b, t: (0, 0))],
        out_specs=pl.BlockSpec((tb, B_WIDTH), lambda b, t: (b * nt + t, 0)),
        out_shape=jax.ShapeDtypeStruct((batch * seq, B_WIDTH), BF16),
        scratch_shapes=[pltpu.VMEM((B_HEADS, B_HEAD_DIM, B_HEAD_DIM), F32),
                        pltpu.VMEM((3, SUBLANES, B_WIDTH), F32),
                        pltpu.VMEM((3, tb, B_WIDTH), F32)],
        compiler_params=_cparams("parallel", "arbitrary"),
        name="deltanet",
    )(proj, proj, proj, proj, p2, conv_w.astype(F32), alog, dtb, norm_g.reshape(1, -1).astype(F32))


def _t5_bucket(dist):
    max_exact = REL_BUCKETS // 2
    n = np.maximum(dist, 0)
    large = max_exact + (np.log(np.maximum(n, 1) / max_exact) / math.log(REL_MAX_DIST / max_exact)
                         * (REL_BUCKETS - max_exact)).astype(np.int32)
    large = np.minimum(large, REL_BUCKETS - 1)
    return np.where(n < max_exact, n, large).astype(np.int32)


def _band_pattern(window, dil):
    i = np.arange(C_BLOCK)[:, None]
    kk = np.arange(2 * C_BLOCK)[None, :]
    j = C_BLOCK + i - kk
    band = (j >= 0) & (j <= window // dil)
    bucket = _t5_bucket(np.clip(j, 0, None) * dil)
    return band, bucket


def _attn_kernel(q_ref, kc_ref, kp_ref, vc_ref, vp_ref, bias_ref, o_ref, lse_ref):
    qb = q_ref.shape[1]
    first = pl.program_id(2) == 0
    scale = C_HEAD_DIM ** -0.5
    for h in range(C_HEADS):
        cols = slice(h * C_HEAD_DIM, (h + 1) * C_HEAD_DIM)
        bias_p = bias_ref[h, :, :C_BLOCK]
        bias_c = bias_ref[h, :, C_BLOCK:]
        for j in range(qb // C_BLOCK):
            rows = slice(j * C_BLOCK, (j + 1) * C_BLOCK)
            q = q_ref[0, rows, cols]
            kc = kc_ref[0, rows, cols]
            vc = vc_ref[0, rows, cols]
            if j == 0:
                kp = kp_ref[0, :, cols]
                vp = vp_ref[0, :, cols]
            else:
                prows = slice((j - 1) * C_BLOCK, j * C_BLOCK)
                kp = kc_ref[0, prows, cols]
                vp = vc_ref[0, prows, cols]
            s_p = _dot_nt(q, kp) * scale + bias_p
            s_c = _dot_nt(q, kc) * scale + bias_c
            if j == 0:
                s_p = jnp.where(first, -jnp.inf, s_p)
            mx = jnp.maximum(jnp.max(s_p, axis=-1, keepdims=True), jnp.max(s_c, axis=-1, keepdims=True))
            p_p = jnp.exp(s_p - mx)
            p_c = jnp.exp(s_c - mx)
            den = jnp.sum(p_p, axis=-1, keepdims=True) + jnp.sum(p_c, axis=-1, keepdims=True)
            num = _dot(p_p.astype(BF16), vp) + _dot(p_c.astype(BF16), vc)
            o_ref[0, rows, cols] = (num / den).astype(o_ref.dtype)
            lse_ref[0, rows, cols] = jnp.broadcast_to(mx + jnp.log(den), (C_BLOCK, C_HEAD_DIM))


def _dilated_attn(proj, rel_bias, gi, batch, seq):
    window, dil = C_PATTERNS[gi]
    sub = seq // dil
    qb = min(QB_ATTN, sub)
    assert sub % qb == 0 and qb % C_BLOCK == 0
    nq = sub // qb
    band, bucket = _band_pattern(window, dil)
    bias = jnp.transpose(rel_bias[bucket][..., gi * C_HEADS:(gi + 1) * C_HEADS], (2, 0, 1)).astype(F32)
    bias = jnp.where(band[None], bias, -jnp.inf)
    view = proj.reshape(batch, sub, dil * PROJ_W)
    qcol, kcol, vcol = CQ_OFF // WBLK + gi, CK_OFF // WBLK + gi, CV_OFF // WBLK
    per = qb // C_BLOCK

    def cur(c):
        return pl.BlockSpec((1, qb, WBLK), lambda b, r, n: (b, n, r * PROJ_WBLKS + c))

    def prev(c):
        return pl.BlockSpec((1, C_BLOCK, WBLK), lambda b, r, n: (b, jnp.maximum(n * per - 1, 0), r * PROJ_WBLKS + c))

    o, lse = pl.pallas_call(
        _attn_kernel,
        grid=(batch, dil, nq),
        in_specs=[cur(qcol), cur(kcol), prev(kcol), cur(vcol), prev(vcol),
                  pl.BlockSpec((C_HEADS, C_BLOCK, 2 * C_BLOCK), lambda b, r, n: (0, 0, 0))],
        out_specs=[pl.BlockSpec((1, qb, C_WIDTH), lambda b, r, n: (b, n, r)),
                   pl.BlockSpec((1, qb, C_WIDTH), lambda b, r, n: (b, n, r))],
        out_shape=[jax.ShapeDtypeStruct((batch, sub, dil * C_WIDTH), BF16),
                   jax.ShapeDtypeStruct((batch, sub, dil * C_WIDTH), F32)],
        compiler_params=_cparams("parallel", "parallel", "arbitrary"),
        name=f"dilated_attn_g{gi}",
    )(view, view, view, view, view, bias)
    return o.reshape(batch * seq, C_WIDTH), lse.reshape(batch * seq, C_WIDTH)


def _merge_kernel(x_ref, g0_ref, g1_ref, g2_ref, ya_ref, yb_ref, o0_ref, o1_ref, o2_ref,
                  l0_ref, l1_ref, l2_ref, wb_ref, wo_ref, out_ref):
    l0, l1, l2 = l0_ref[...], l1_ref[...], l2_ref[...]
    lm = jnp.maximum(jnp.maximum(l0, l1), l2)
    e0, e1, e2 = jnp.exp(l0 - lm), jnp.exp(l1 - lm), jnp.exp(l2 - lm)
    yc = (e0 * o0_ref[...].astype(F32) + e1 * o1_ref[...].astype(F32) + e2 * o2_ref[...].astype(F32)) / (e0 + e1 + e2)
    merged = (_sigmoid(g0_ref[...].astype(F32)) * _dot(ya_ref[...], wb_ref[0])
              + _sigmoid(g1_ref[...].astype(F32)) * _dot(yb_ref[...], wb_ref[1])
              + _sigmoid(g2_ref[...].astype(F32)) * _dot(yc.astype(BF16), wb_ref[2]))
    out_ref[...] = x_ref[...] + _dot(merged.astype(BF16), wo_ref[...])


def _merge(x2d, proj, ya, yb, attn, w_branch, w_out):
    n, d = x2d.shape
    tm = min(TM_MERGE, n)
    row = lambda w: pl.BlockSpec((tm, w), lambda i: (i, 0))
    gate = lambda g: pl.BlockSpec((tm, d), lambda i: (i, GATE_OFF // d + g))
    (o0, l0), (o1, l1), (o2, l2) = attn
    return pl.pallas_call(
        _merge_kernel,
        grid=(n // tm,),
        in_specs=[row(d), gate(0), gate(1), gate(2), row(WBLK), row(WBLK),
                  row(WBLK), row(WBLK), row(WBLK), row(WBLK), row(WBLK), row(WBLK),
                  pl.BlockSpec((3, WBLK, d), lambda i: (0, 0, 0)),
                  pl.BlockSpec((d, d), lambda i: (0, 0))],
        out_specs=row(d),
        out_shape=jax.ShapeDtypeStruct((n, d), F32),
        compiler_params=_cparams("parallel"),
        name="merge",
    )(x2d, proj, proj, proj, ya, yb, o0, o1, o2, l0, l1, l2, w_branch.astype(BF16), w_out.astype(BF16))


EXP_LANE0 = MOE_GROUPS
META_E, META_RANK, META_GATE = 0, 2, 4


def _split3_dot(h, w_hi, w_lo):
    h_hi = h.astype(BF16)
    h_lo = (h - h_hi.astype(F32)).astype(BF16)
    return _dot(h_hi, w_hi) + (_dot(h_lo, w_hi) + _dot(h_hi, w_lo))


def _router_kernel(x_ref, g_ref, whi_ref, wlo_ref, meta_ref, cnt_ref, run_ref):
    tm = x_ref.shape[0]

    @pl.when(pl.program_id(0) == 0)
    def _():
        run_ref[...] = jnp.zeros_like(run_ref)

    x = x_ref[...]
    h = x * lax.rsqrt(jnp.mean(x * x, axis=-1, keepdims=True) + EPS) * g_ref[...]
    logits = _split3_dot(h, whi_ref[...], wlo_ref[...])
    lane = lax.broadcasted_iota(jnp.int32, (tm, LANES), 1)
    lane_f = lane.astype(F32)
    big = float(LANES)

    def first_argmax(vals, mx):
        return jnp.min(jnp.where(vals == mx, lane_f, big), axis=-1, keepdims=True)

    gl = jnp.where(lane < MOE_GROUPS, logits, -jnp.inf)
    gmax = jnp.max(gl, axis=-1, keepdims=True)
    grp_p = 1.0 / jnp.sum(jnp.exp(gl - gmax), axis=-1, keepdims=True)
    gidx = first_argmax(gl, gmax)
    lo = EXP_LANE0 + gidx * MOE_PER_GROUP
    in_grp = (lane_f >= lo) & (lane_f < lo + MOE_PER_GROUP)
    el = jnp.where(in_grp, logits, -jnp.inf)
    m1 = jnp.max(el, axis=-1, keepdims=True)
    i1 = first_argmax(el, m1)
    el2 = jnp.where(lane_f == i1, -jnp.inf, el)
    m2 = jnp.max(el2, axis=-1, keepdims=True)
    i2 = first_argmax(el2, m2)
    t2 = jnp.exp(m2 - m1)
    gate1 = grp_p / (1.0 + t2)
    gate2 = grp_p * t2 / (1.0 + t2)
    oh1 = lane_f == i1
    oh2 = lane_f == i2
    onehot = jnp.where(oh1 | oh2, 1.0, 0.0)
    ri = lax.broadcasted_iota(jnp.int32, (tm, tm), 0)
    ci = lax.broadcasted_iota(jnp.int32, (tm, tm), 1)
    before = jnp.where(ri > ci, 1.0, 0.0).astype(BF16)
    prior = _dot(before, onehot.astype(BF16)) + run_ref[0:1, :]
    rank1 = jnp.sum(jnp.where(oh1, prior, 0.0), axis=-1, keepdims=True)
    rank2 = jnp.sum(jnp.where(oh2, prior, 0.0), axis=-1, keepdims=True)
    run_new = run_ref[0:1, :] + jnp.sum(onehot, axis=0, keepdims=True)
    run_ref[0:1, :] = run_new
    cnt_ref[...] = jnp.broadcast_to(run_new, cnt_ref.shape)
    rec = jnp.zeros((tm, LANES), F32)
    for ln, val in ((META_E, i1 - EXP_LANE0), (META_E + 1, i2 - EXP_LANE0), (META_RANK, rank1),
                    (META_RANK + 1, rank2), (META_GATE, gate1), (META_GATE + 1, gate2)):
        rec = jnp.where(lane == ln, val, rec)
    meta_ref[...] = rec


def _router(x2d, g, w_group, w_expert):
    n, d = x2d.shape
    tm = min(TM_ROUTER, n)
    w = jnp.concatenate([w_group, w_expert], axis=1).astype(F32)
    w = jnp.pad(w, ((0, 0), (0, LANES - w.shape[1])))
    w_hi = w.astype(BF16)
    w_lo = (w - w_hi.astype(F32)).astype(BF16)
    return pl.pallas_call(
        _router_kernel,
        grid=(n // tm,),
        in_specs=[pl.BlockSpec((tm, d), lambda i: (i, 0)),
                  pl.BlockSpec((1, d), lambda i: (0, 0)),
                  pl.BlockSpec((d, LANES), lambda i: (0, 0)),
                  pl.BlockSpec((d, LANES), lambda i: (0, 0))],
        out_specs=[pl.BlockSpec((tm, LANES), lambda i: (i, 0)),
                   pl.BlockSpec((SUBLANES, LANES), lambda i: (0, 0))],
        out_shape=[jax.ShapeDtypeStruct((n, LANES), F32),
                   jax.ShapeDtypeStruct((SUBLANES, LANES), F32)],
        scratch_shapes=[pltpu.VMEM((SUBLANES, LANES), F32)],
        compiler_params=_cparams("arbitrary"),
        name="router",
    )(x2d, g.reshape(1, d), w_hi, w_lo)


def _dispatch_kernel(dest_ref, x_hbm, xs_init_hbm, xs_hbm, sem):
    del xs_init_hbm
    td = dest_ref.shape[0] // 2
    base = pl.program_id(0) * td

    def row_copy(i, k):
        return pltpu.make_async_copy(x_hbm.at[pl.ds(base + i, 1)], xs_hbm.at[pl.ds(dest_ref[2 * i + k], 1)], sem)

    def start(i, c):
        row_copy(i, 0).start()
        row_copy(i, 1).start()
        return c

    def wait(i, c):
        row_copy(i, 0).wait()
        row_copy(i, 1).wait()
        return c

    lax.fori_loop(0, td, start, 0)
    lax.fori_loop(0, td, wait, 0)


def _dispatch(x2d, dest, rows):
    n, d = x2d.shape
    td = min(TD_DISPATCH, n)
    return pl.pallas_call(
        _dispatch_kernel,
        grid=(n // td,),
        in_specs=[pl.BlockSpec((2 * td,), lambda i: (i,), memory_space=pltpu.SMEM),
                  pl.BlockSpec(memory_space=pl.ANY),
                  pl.BlockSpec(memory_space=pl.ANY)],
        out_specs=pl.BlockSpec(memory_space=pl.ANY),
        out_shape=jax.ShapeDtypeStruct((rows, d), F32),
        scratch_shapes=[pltpu.SemaphoreType.DMA],
        input_output_aliases={2: 0},
        compiler_params=_cparams("arbitrary"),
        name="moe_dispatch",
    )(dest, x2d, jnp.zeros((rows, d), F32))


def _experts_kernel(be_ref, nused_ref, xs_ref, g_ref, w1_ref, w3_ref, w2_ref, ys_ref):
    del be_ref
    i = pl.program_id(0)

    @pl.when(i < nused_ref[0])
    def _():
        x = xs_ref[...]
        h = (x * lax.rsqrt(jnp.mean(x * x, axis=-1, keepdims=True) + EPS) * g_ref[...]).astype(BF16)
        hid = _silu(_dot(h, w1_ref[0])) * _dot(h, w3_ref[0])
        ys_ref[...] = _dot(hid.astype(BF16), w2_ref[0])

    @pl.when(i >= nused_ref[0])
    def _():
        ys_ref[...] = jnp.zeros_like(ys_ref)


def _experts(xs, g, blk_expert, n_used, w1, w3, w2):
    rows, d = xs.shape
    nb = rows // BM_EXPERT
    ff = w1.shape[-1]
    grid_spec = pltpu.PrefetchScalarGridSpec(
        num_scalar_prefetch=2,
        grid=(nb,),
        in_specs=[pl.BlockSpec((BM_EXPERT, d), lambda i, be, nu: (i, 0)),
                  pl.BlockSpec((1, d), lambda i, be, nu: (0, 0)),
                  pl.BlockSpec((1, d, ff), lambda i, be, nu: (be[i], 0, 0)),
                  pl.BlockSpec((1, d, ff), lambda i, be, nu: (be[i], 0, 0)),
                  pl.BlockSpec((1, ff, d), lambda i, be, nu: (be[i], 0, 0))],
        out_specs=pl.BlockSpec((BM_EXPERT, d), lambda i, be, nu: (i, 0)),
    )
    return pl.pallas_call(
        _experts_kernel,
        grid_spec=grid_spec,
        out_shape=jax.ShapeDtypeStruct((rows, d), F32),
        compiler_params=_cparams("arbitrary"),
        name="moe_experts",
    )(blk_expert, n_used, xs, g.reshape(1, d), w1.astype(BF16), w3.astype(BF16), w2.astype(BF16))


def _combine_kernel(final, dest_ref, x_ref, meta_ref, ys_hbm, g_ref, o_ref, ybuf, sem):
    tc = x_ref.shape[0]

    def row_copy(i, k):
        return pltpu.make_async_copy(ys_hbm.at[pl.ds(dest_ref[2 * i + k], 1)], ybuf.at[k, pl.ds(i, 1)], sem)

    def start(i, c):
        row_copy(i, 0).start()
        row_copy(i, 1).start()
        return c

    def wait(i, c):
        row_copy(i, 0).wait()
        row_copy(i, 1).wait()
        return c

    lax.fori_loop(0, tc, start, 0)
    lax.fori_loop(0, tc, wait, 0)
    meta = meta_ref[...]
    y = (x_ref[...] + meta[:, META_GATE:META_GATE + 1] * ybuf[0]
         + meta[:, META_GATE + 1:META_GATE + 2] * ybuf[1])
    if final:
        y = y * lax.rsqrt(jnp.mean(y * y, axis=-1, keepdims=True) + EPS) * g_ref[...]
    o_ref[...] = y


def _combine(x2d, meta, dest, ys, final_g, final):
    n, d = x2d.shape
    tc = min(TC_COMBINE, n)
    return pl.pallas_call(
        functools.partial(_combine_kernel, final),
        grid=(n // tc,),
        in_specs=[pl.BlockSpec((2 * tc,), lambda i: (i,), memory_space=pltpu.SMEM),
                  pl.BlockSpec((tc, d), lambda i: (i, 0)),
                  pl.BlockSpec((tc, LANES), lambda i: (i, 0)),
                  pl.BlockSpec(memory_space=pl.ANY),
                  pl.BlockSpec((1, d), lambda i: (0, 0))],
        out_specs=pl.BlockSpec((tc, d), lambda i: (i, 0)),
        out_shape=jax.ShapeDtypeStruct((n, d), F32),
        scratch_shapes=[pltpu.VMEM((2, tc, d), F32), pltpu.SemaphoreType.DMA],
        compiler_params=_cparams("arbitrary"),
        name="moe_combine_final" if final else "moe_combine",
    )(dest, x2d, meta, ys, final_g.reshape(1, d))


def _moe(x2d, norm_g, w_group, w_expert, w1, w3, w2, final_g, final):
    n, d = x2d.shape
    meta, cnt = _router(x2d, norm_g, w_group, w_expert)
    counts = cnt[0, EXP_LANE0:EXP_LANE0 + MOE_EXPERTS].astype(jnp.int32)
    padded = (counts + BM_EXPERT - 1) // BM_EXPERT * BM_EXPERT
    pad_end = jnp.cumsum(padded)
    pad_start = pad_end - padded
    e = meta[:, META_E:META_E + 2].astype(jnp.int32)
    rank = meta[:, META_RANK:META_RANK + 2].astype(jnp.int32)
    dest = (pad_start[e] + rank).reshape(2 * n)
    rows = 2 * n + MOE_EXPERTS * BM_EXPERT
    nb = rows // BM_EXPERT
    blk_expert = jnp.minimum(jnp.searchsorted(pad_end, jnp.arange(nb, dtype=jnp.int32) * BM_EXPERT, side='right'),
                             MOE_EXPERTS - 1).astype(jnp.int32)
    n_used = (pad_end[-1:] // BM_EXPERT).astype(jnp.int32)
    xs = _dispatch(x2d, dest, rows)
    ys = _experts(xs, norm_g, blk_expert, n_used, w1, w3, w2)
    return _combine(x2d, meta, dest, ys, final_g, final)


def _split_w_in(w):
    n_ab = 2 * A_WIDTH + 4 * B_WIDTH
    n_c = n_ab + 2 * B_HEADS
    n_g = n_c + 2 * 3 * C_WIDTH + C_WIDTH
    w_main = jnp.concatenate([w[:, n_g:], w[:, :n_ab], w[:, n_c:n_g]], axis=1).astype(BF16)
    w_small = jnp.pad(w[:, n_ab:n_c], ((0, 0), (0, LANES - 2 * B_HEADS))).astype(BF16)
    return w_main, w_small


def kernel(x, norm_mix_g, w_in, a_ln_g, a_ln_b, a_w_s, a_b_s, b_conv_w, b_a_log, b_dt_bias, b_norm_g, rel_bias,
           w_branch, w_out, norm_ffn_g, w_group, w_expert, w1, w3, w2, final_norm_g):
    batch, seq, d = x.shape
    depth = w_in.shape[0]
    x2d = x.reshape(batch * seq, d)
    for l in range(depth):
        w_main, w_small = _split_w_in(w_in[l])
        assert w_main.shape[1] == PROJ_W
        proj = _rms_matmul(x2d, norm_mix_g[l], w_main, BF16, TN_PROJ)
        p2 = _rms_matmul(x2d, norm_mix_g[l], w_small, F32, LANES)
        ya = _gmlp(proj, a_w_s[l], a_b_s[l], a_ln_g[l], a_ln_b[l])
        yb = _deltanet(proj, p2, b_conv_w[l], b_a_log[l], b_dt_bias[l], b_norm_g[l], batch, seq)
        attn = [_dilated_attn(proj, rel_bias, gi, batch, seq) for gi in range(len(C_PATTERNS))]
        x2d = _merge(x2d, proj, ya, yb, attn, w_branch[l], w_out[l])
        x2d = _moe(x2d, norm_ffn_g[l], w_group[l], w_expert[l], w1[l], w3[l], w2[l], final_norm_g,
                   final=(l == depth - 1))
    return x2d.reshape(batch, seq, d)
```

```python
import functools
import math

import numpy as np
import jax
import jax.numpy as jnp
from jax import lax
from jax.experimental import pallas as pl
from jax.experimental.pallas import tpu as pltpu

F32 = jnp.float32
BF16 = jnp.bfloat16

D_MODEL = 1024
A_GROUPS, A_CHUNK, A_WIDTH = 4, 128, 512
B_HEADS, B_HEAD_DIM, B_CONV, B_WIDTH = 4, 128, 4, 512
C_PATTERNS = ((128, 1), (512, 4), (2048, 16))
C_HEADS, C_HEAD_DIM, C_BLOCK, C_WIDTH = 4, 128, 128, 512
REL_BUCKETS, REL_MAX_DIST = 32, 2048
MOE_GROUPS, MOE_PER_GROUP, MOE_EXPERTS, MOE_FF = 4, 8, 32, 512
EPS = 1e-6

LANES = 128
SUBLANES = 8
VMEM_LIMIT = 48 * 1024 * 1024

PROJ_W = 6144
GATE_OFF, AU_OFF, AV_OFF = 0, 3072, 3584
BQ_OFF, BK_OFF, BV_OFF, BZ_OFF = 4096, 4608, 5120, 5632
WBLK = 512
CPROJ_W = 3584
CQ_OFF, CK_OFF, CV_OFF = 0, 1536, 3072

TM_PROJ, TN_PROJ = 1024, 512
TM_GMLP = 512
TB_DELTA = 512
CHUNK = 128
TT_ATTN = 2048
TM_MERGE = 256
TM_ROUTER = 512
TD_DISPATCH = 512
BM_EXPERT = 256
TC_COMBINE = 256


def _cparams(*sem):
    return pltpu.CompilerParams(dimension_semantics=sem, vmem_limit_bytes=VMEM_LIMIT)


def _dot(a, b):
    return jnp.dot(a, b, preferred_element_type=F32)


def _dot_nt(a, b):
    return lax.dot_general(a, b, (((1,), (1,)), ((), ())), preferred_element_type=F32)


def _sigmoid(x):
    return 1.0 / (1.0 + jnp.exp(-x))


def _silu(x):
    return x * _sigmoid(x)


def _gelu_tanh(x):
    c = math.sqrt(2.0 / math.pi)
    return x * (0.5 * (1.0 + jnp.tanh(c * (x + 0.044715 * (x * x * x)))))


def _softplus(x):
    return jnp.maximum(x, 0.0) + jnp.log(1.0 + jnp.exp(-jnp.abs(x)))


def _rms_matmul_kernel(x_ref, g_ref, w_ref, o_ref, h_ref):
    @pl.when(pl.program_id(1) == 0)
    def _():
        x = x_ref[...]
        ms = jnp.mean(x * x, axis=-1, keepdims=True)
        h_ref[...] = (x * lax.rsqrt(ms + EPS) * g_ref[...]).astype(h_ref.dtype)

    o_ref[...] = _dot(h_ref[...], w_ref[...]).astype(o_ref.dtype)


def _rms_matmul(x2d, g, w, out_dtype, tn):
    n, d = x2d.shape
    ncols = w.shape[1]
    tm = min(TM_PROJ, n)
    return pl.pallas_call(
        _rms_matmul_kernel,
        grid=(n // tm, ncols // tn),
        in_specs=[pl.BlockSpec((tm, d), lambda i, j: (i, 0)),
                  pl.BlockSpec((1, d), lambda i, j: (0, 0)),
                  pl.BlockSpec((d, tn), lambda i, j: (0, j))],
        out_specs=pl.BlockSpec((tm, tn), lambda i, j: (i, j)),
        out_shape=jax.ShapeDtypeStruct((n, ncols), out_dtype),
        scratch_shapes=[pltpu.VMEM((tm, d), BF16)],
        compiler_params=_cparams("parallel", "arbitrary"),
        name="rms_matmul",
    )(x2d, g.reshape(1, d), w)


def _gmlp_kernel(u_ref, v_ref, w_ref, b_ref, lng_ref, lnb_ref, o_ref):
    tm = u_ref.shape[0]
    u = _gelu_tanh(u_ref[...].astype(F32))
    v = _gelu_tanh(v_ref[...].astype(F32))
    mu = jnp.mean(v, axis=-1, keepdims=True)
    vc = v - mu
    var = jnp.mean(vc * vc, axis=-1, keepdims=True)
    vn = (vc * lax.rsqrt(var + EPS) * lng_ref[...] + lnb_ref[...]).astype(BF16)
    for c in range(tm // A_CHUNK):
        rows = slice(c * A_CHUNK, (c + 1) * A_CHUNK)
        for g in range(A_GROUPS):
            cols = slice(g * LANES, (g + 1) * LANES)
            sv = _dot(w_ref[g], vn[rows, cols]) + b_ref[:, cols]
            o_ref[rows, cols] = (u[rows, cols] * sv).astype(o_ref.dtype)


def _gmlp(proj, w_s, b_s, ln_g, ln_b):
    n = proj.shape[0]
    tm = min(TM_GMLP, n)
    causal = np.tril(np.ones((A_CHUNK, A_CHUNK), dtype=bool))
    w = jnp.where(causal, w_s, 0.0).astype(BF16)
    bias = jnp.repeat(jnp.transpose(b_s), LANES, axis=1).astype(F32)
    return pl.pallas_call(
        _gmlp_kernel,
        grid=(n // tm,),
        in_specs=[pl.BlockSpec((tm, WBLK), lambda i: (i, AU_OFF // WBLK)),
                  pl.BlockSpec((tm, WBLK), lambda i: (i, AV_OFF // WBLK)),
                  pl.BlockSpec((A_GROUPS, A_CHUNK, A_CHUNK), lambda i: (0, 0, 0)),
                  pl.BlockSpec((A_CHUNK, A_WIDTH), lambda i: (0, 0)),
                  pl.BlockSpec((1, A_WIDTH), lambda i: (0, 0)),
                  pl.BlockSpec((1, A_WIDTH), lambda i: (0, 0))],
        out_specs=pl.BlockSpec((tm, A_WIDTH), lambda i: (i, 0)),
        out_shape=jax.ShapeDtypeStruct((n, A_WIDTH), BF16),
        compiler_params=_cparams("parallel"),
        name="gmlp",
    )(proj, proj, w, bias, ln_g.reshape(1, -1), ln_b.reshape(1, -1))


def _deltanet_kernel(q_ref, k_ref, v_ref, z_ref, p2_ref, cw_ref, alog_ref, dtb_ref, ng_ref, o_ref,
                     s_ref, carry_ref, qkv_ref):
    tb = q_ref.shape[0]
    nchunk = tb // CHUNK

    @pl.when(pl.program_id(1) == 0)
    def _():
        s_ref[...] = jnp.zeros_like(s_ref)
        carry_ref[...] = jnp.zeros_like(carry_ref)

    row8 = lax.broadcasted_iota(jnp.int32, (SUBLANES, B_WIDTH), 0)
    for idx, ref in enumerate((q_ref, k_ref, v_ref)):
        x = ref[...].astype(F32)
        w = cw_ref[:, idx * B_WIDTH:(idx + 1) * B_WIDTH]
        prev = carry_ref[idx]
        acc = x * w[B_CONV - 1:B_CONV]
        for s in range(1, B_CONV):
            xs = pltpu.roll(x, s, axis=0)
            ps = pltpu.roll(prev, s, axis=0)
            head = jnp.where(row8 < s, ps, xs[:SUBLANES])
            xs = jnp.concatenate([head, xs[SUBLANES:]], axis=0)
            acc = acc + xs * w[B_CONV - 1 - s:B_CONV - s]
        carry_ref[idx] = x[tb - SUBLANES:]
        qkv_ref[idx] = _silu(acc)

    ri = lax.broadcasted_iota(jnp.int32, (CHUNK, CHUNK), 0)
    ci = lax.broadcasted_iota(jnp.int32, (CHUNK, CHUNK), 1)
    incl = ri >= ci
    strict = ri > ci
    tril_f = incl.astype(F32)
    alog = alog_ref[...]
    dtb = dtb_ref[...]
    ng = ng_ref[...]

    def chunk_body(c, carry):
        r0 = pl.multiple_of(c * CHUNK, CHUNK)
        rows = pl.ds(r0, CHUNK)
        p2 = p2_ref[rows, :]
        g_all = -jnp.exp(alog) * _softplus(p2 + dtb)
        gc_all = jnp.dot(tril_f, g_all, preferred_element_type=F32, precision=lax.Precision.HIGHEST)
        gc_t = gc_all.T
        beta_all = _sigmoid(p2)
        for h in range(B_HEADS):
            cols = slice(h * B_HEAD_DIM, (h + 1) * B_HEAD_DIM)
            q = qkv_ref[0, rows, cols]
            k = qkv_ref[1, rows, cols]
            v = qkv_ref[2, rows, cols]
            q = q * (lax.rsqrt(jnp.sum(q * q, axis=-1, keepdims=True) + EPS) * (B_HEAD_DIM ** -0.5))
            k = k * lax.rsqrt(jnp.sum(k * k, axis=-1, keepdims=True) + EPS)
            beta = beta_all[:, h:h + 1]
            gc_col = jnp.broadcast_to(gc_all[:, B_HEADS + h:B_HEADS + h + 1], (CHUNK, B_HEAD_DIM))
            gc_row = gc_t[B_HEADS + h:B_HEADS + h + 1, :]
            decay = jnp.exp(jnp.where(incl, gc_col - gc_row, -jnp.inf))
            kb = k * beta
            vb = v * beta
            egc = jnp.exp(gc_col)
            kbf = k.astype(BF16)
            aq = _dot_nt(jnp.concatenate([kb, q], axis=0).astype(BF16), kbf)
            low = jnp.where(strict, aq[:CHUNK] * decay, 0.0)
            attn = aq[CHUNK:] * decay
            m = -low
            mb = m.astype(BF16)
            p = _dot(mb, mb)
            r = m
            for _ in range(5):
                rp = _dot(jnp.concatenate([r, p], axis=0).astype(BF16), p.astype(BF16))
                r = r + p + rp[:CHUNK]
                p = rp[CHUNK:]
            r = r + p + _dot(r.astype(BF16), p.astype(BF16))
            rhs = jnp.concatenate([vb, kb * egc], axis=1)
            sol = rhs + _dot(r.astype(BF16), rhs.astype(BF16))
            u = sol[:, :B_HEAD_DIM]
            w = sol[:, B_HEAD_DIM:]
            q_dec = q * egc
            g_last = gc_col[CHUNK - 1:CHUNK, :]
            k_dec = k * jnp.exp(g_last - gc_col)
            s_old = s_ref[h]
            wq = _dot(jnp.concatenate([w, q_dec], axis=0).astype(BF16), s_old.astype(BF16))
            v_new = u - wq[:CHUNK]
            vnb = v_new.astype(BF16)
            o = wq[CHUNK:] + _dot(attn.astype(BF16), vnb)
            s_ref[h] = s_old * jnp.exp(g_last) + _dot(k_dec.T.astype(BF16), vnb)
            z = z_ref[rows, cols].astype(F32)
            on = o * lax.rsqrt(jnp.mean(o * o, axis=-1, keepdims=True) + EPS) * ng
            o_ref[rows, cols] = (on * _silu(z)).astype(o_ref.dtype)
        return carry

    lax.fori_loop(0, nchunk, chunk_body, 0)


def _deltanet(proj, p2, conv_w, a_log, dt_bias, norm_g, batch, seq):
    tb = min(TB_DELTA, seq)
    nt = seq // tb

    def col(off):
        return pl.BlockSpec((tb, WBLK), lambda b, t: (b * nt + t, off // WBLK))

    pad = LANES - 2 * B_HEADS
    alog = jnp.pad(a_log.astype(F32), (B_HEADS, pad)).reshape(1, LANES)
    dtb = jnp.pad(dt_bias.astype(F32), (B_HEADS, pad)).reshape(1, LANES)
    return pl.pallas_call(
        _deltanet_kernel,
        grid=(batch, nt),
        in_specs=[col(BQ_OFF), col(BK_OFF), col(BV_OFF), col(BZ_OFF),
                  pl.BlockSpec((tb, LANES), lambda b, t: (b * nt + t, 0)),
                  pl.BlockSpec((B_CONV, 3 * B_WIDTH), lambda b, t: (0, 0)),
                  pl.BlockSpec((1, LANES), lambda b, t: (0, 0)),
                  pl.BlockSpec((1, LANES), lambda b, t: (0, 0)),
                  pl.BlockSpec((1, B_HEAD_DIM), lambda b, t: (0, 0))],
        out_specs=pl.BlockSpec((tb, B_WIDTH), lambda b, t: (b * nt + t, 0)),
        out_shape=jax.ShapeDtypeStruct((batch * seq, B_WIDTH), BF16),
        scratch_shapes=[pltpu.VMEM((B_HEADS, B_HEAD_DIM, B_HEAD_DIM), F32),
                        pltpu.VMEM((3, SUBLANES, B_WIDTH), F32),
                        pltpu.VMEM((3, tb, B_WIDTH), F32)],
        compiler_params=_cparams("parallel", "arbitrary"),
        name="deltanet",
    )(proj, proj, proj, proj, p2, conv_w.astype(F32), alog, dtb, norm_g.reshape(1, -1).astype(F32))


def _t5_bucket(dist):
    max_exact = REL_BUCKETS // 2
    n = np.maximum(dist, 0)
    large = max_exact + (np.log(np.maximum(n, 1) / max_exact) / math.log(REL_MAX_DIST / max_exact)
                         * (REL_BUCKETS - max_exact)).astype(np.int32)
    large = np.minimum(large, REL_BUCKETS - 1)
    return np.where(n < max_exact, n, large).astype(np.int32)


def _band_pattern(window, dil):
    i = np.arange(C_BLOCK)[:, None]
    kk = np.arange(2 * C_BLOCK)[None, :]
    j = C_BLOCK + i - kk
    band = (j >= 0) & (j <= window // dil)
    bucket = _t5_bucket(np.clip(j, 0, None) * dil)
    return band, bucket


def _attn_block(q, kp, kc, vp, vc, bias_p, bias_c, no_prev):
    scale = C_HEAD_DIM ** -0.5
    qb = q.astype(BF16)
    s_p = _dot_nt(qb, kp.astype(BF16)) * scale + bias_p
    s_c = _dot_nt(qb, kc.astype(BF16)) * scale + bias_c
    if no_prev is not None:
        s_p = jnp.where(no_prev, -jnp.inf, s_p)
    mx = jnp.maximum(jnp.max(s_p, axis=-1, keepdims=True), jnp.max(s_c, axis=-1, keepdims=True))
    p_p = jnp.exp(s_p - mx)
    p_c = jnp.exp(s_c - mx)
    den = jnp.sum(p_p, axis=-1, keepdims=True) + jnp.sum(p_c, axis=-1, keepdims=True)
    num = _dot(p_p.astype(BF16), vp.astype(BF16)) + _dot(p_c.astype(BF16), vc.astype(BF16))
    return mx, den, num


def _attn_kernel(q0_ref, q1_ref, q2_ref, k0_ref, k1_ref, k2_ref, kp0_ref, kp1_ref, kp2_ref, v_ref, vp_ref,
                 bias_ref, o_ref, m_ref, l_ref, acc_ref):
    tt = q0_ref.shape[0]
    first = pl.program_id(1) == 0
    q_refs = (q0_ref, q1_ref, q2_ref)
    k_refs = (k0_ref, k1_ref, k2_ref)
    kp_refs = (kp0_ref, kp1_ref, kp2_ref)
    shape = (C_BLOCK, C_HEAD_DIM)

    for g, (_, dil) in enumerate(C_PATTERNS):
        q_ref, k_ref, kp_ref = q_refs[g], k_refs[g], kp_refs[g]
        span = C_BLOCK * dil
        nblk = tt // span
        bias_p = bias_ref[g, 0, :, :C_BLOCK]
        bias_c = bias_ref[g, 0, :, C_BLOCK:]

        def rows(start, dil=dil):
            if dil > 1:
                return pl.ds(start, C_BLOCK, stride=dil)
            return pl.ds(start if isinstance(start, int) else pl.multiple_of(start, C_BLOCK), C_BLOCK)

        def update(start, mx, den, num, g=g, rows=rows):
            sl = rows(start)
            if g == 0:
                m_ref[sl, :] = jnp.broadcast_to(mx, shape)
                l_ref[sl, :] = jnp.broadcast_to(den, shape)
                acc_ref[sl, :] = num
            else:
                m_old = m_ref[sl, :]
                m_new = jnp.maximum(m_old, mx)
                a = jnp.exp(m_old - m_new)
                b = jnp.exp(mx - m_new)
                l_ref[sl, :] = a * l_ref[sl, :] + b * den
                acc_ref[sl, :] = a * acc_ref[sl, :] + b * num
                m_ref[sl, :] = m_new

        def head_block(r, c, q_ref=q_ref, k_ref=k_ref, kp_ref=kp_ref, rows=rows, update=update,
                       bias_p=bias_p, bias_c=bias_c, span=span, nblk=nblk):
            sl = rows(r)
            pl_ = rows(r + (nblk - 1) * span)
            mx, den, num = _attn_block(q_ref[sl, :], kp_ref[pl_, :], k_ref[sl, :], vp_ref[pl_, :], v_ref[sl, :],
                                       bias_p, bias_c, first)
            update(r, mx, den, num)
            return c

        def inner_block(i, c, q_ref=q_ref, k_ref=k_ref, rows=rows, update=update,
                        bias_p=bias_p, bias_c=bias_c, span=span, nblk=nblk, dil=dil):
            r = i // (nblk - 1)
            n = i % (nblk - 1) + 1
            start = r + n * span
            sl = rows(start)
            pl_ = rows(start - span)
            mx, den, num = _attn_block(q_ref[sl, :], k_ref[pl_, :], k_ref[sl, :], v_ref[pl_, :], v_ref[sl, :],
                                       bias_p, bias_c, None)
            update(start, mx, den, num)
            return c

        if dil > 1:
            lax.fori_loop(0, dil, head_block, 0)
        else:
            head_block(0, 0)
        if nblk > 1:
            lax.fori_loop(0, dil * (nblk - 1), inner_block, 0)

    o_ref[...] = (acc_ref[...] / l_ref[...]).astype(o_ref.dtype)


def _dilated_attn(cproj, rel_bias, batch, seq):
    tt = TT_ATTN
    assert seq % tt == 0 and tt == C_BLOCK * max(d for _, d in C_PATTERNS)
    nt = seq // tt
    biases = []
    for gi, (window, dil) in enumerate(C_PATTERNS):
        band, bucket = _band_pattern(window, dil)
        bias = jnp.transpose(rel_bias[bucket][..., gi * C_HEADS:(gi + 1) * C_HEADS], (2, 0, 1)).astype(F32)
        biases.append(jnp.where(band[None], bias, -jnp.inf))
    bias = jnp.stack(biases)

    def cur(off, g=0):
        c0 = off // C_HEAD_DIM + g * C_HEADS
        return pl.BlockSpec((tt, C_HEAD_DIM), lambda b, t, h: (b * nt + t, c0 + h))

    def prev(off, g=0):
        c0 = off // C_HEAD_DIM + g * C_HEADS
        return pl.BlockSpec((tt, C_HEAD_DIM), lambda b, t, h: (b * nt + jnp.maximum(t - 1, 0), c0 + h))

    ng = len(C_PATTERNS)
    in_specs = ([cur(CQ_OFF, g) for g in range(ng)] + [cur(CK_OFF, g) for g in range(ng)]
                + [prev(CK_OFF, g) for g in range(ng)] + [cur(CV_OFF), prev(CV_OFF)]
                + [pl.BlockSpec((ng, 1, C_BLOCK, 2 * C_BLOCK), lambda b, t, h: (0, h, 0, 0))])
    return pl.pallas_call(
        _attn_kernel,
        grid=(batch, nt, C_HEADS),
        in_specs=in_specs,
        out_specs=pl.BlockSpec((tt, C_HEAD_DIM), lambda b, t, h: (b * nt + t, h)),
        out_shape=jax.ShapeDtypeStruct((batch * seq, C_WIDTH), BF16),
        scratch_shapes=[pltpu.VMEM((tt, C_HEAD_DIM), F32)] * 3,
        compiler_params=_cparams("parallel", "arbitrary", "arbitrary"),
        name="dilated_attn",
    )(*([cproj] * (3 * ng + 2)), bias)


def _merge_kernel(x_ref, g0_ref, g1_ref, g2_ref, ya_ref, yb_ref, yc_ref, wb_ref, wo_ref, out_ref):
    merged = (_sigmoid(g0_ref[...].astype(F32)) * _dot(ya_ref[...], wb_ref[0])
              + _sigmoid(g1_ref[...].astype(F32)) * _dot(yb_ref[...], wb_ref[1])
              + _sigmoid(g2_ref[...].astype(F32)) * _dot(yc_ref[...], wb_ref[2]))
    out_ref[...] = x_ref[...] + _dot(merged.astype(BF16), wo_ref[...])


def _merge(x2d, proj, ya, yb, yc, w_branch, w_out):
    n, d = x2d.shape
    tm = min(TM_MERGE, n)
    row = lambda w: pl.BlockSpec((tm, w), lambda i: (i, 0))
    gate = lambda g: pl.BlockSpec((tm, d), lambda i: (i, GATE_OFF // d + g))
    return pl.pallas_call(
        _merge_kernel,
        grid=(n // tm,),
        in_specs=[row(d), gate(0), gate(1), gate(2), row(WBLK), row(WBLK), row(WBLK),
                  pl.BlockSpec((3, WBLK, d), lambda i: (0, 0, 0)),
                  pl.BlockSpec((d, d), lambda i: (0, 0))],
        out_specs=row(d),
        out_shape=jax.ShapeDtypeStruct((n, d), F32),
        compiler_params=_cparams("parallel"),
        name="merge",
    )(x2d, proj, proj, proj, ya, yb, yc, w_branch.astype(BF16), w_out.astype(BF16))


EXP_LANE0 = MOE_GROUPS
META_E, META_RANK, META_GATE = 0, 2, 4


def _split3_dot(h, w_hi, w_lo):
    h_hi = h.astype(BF16)
    h_lo = (h - h_hi.astype(F32)).astype(BF16)
    return _dot(h_hi, w_hi) + (_dot(h_lo, w_hi) + _dot(h_hi, w_lo))


def _router_kernel(x_ref, g_ref, whi_ref, wlo_ref, meta_ref, cnt_ref, run_ref):
    tm = x_ref.shape[0]

    @pl.when(pl.program_id(0) == 0)
    def _():
        run_ref[...] = jnp.zeros_like(run_ref)

    x = x_ref[...]
    h = x * lax.rsqrt(jnp.mean(x * x, axis=-1, keepdims=True) + EPS) * g_ref[...]
    logits = _split3_dot(h, whi_ref[...], wlo_ref[...])
    lane = lax.broadcasted_iota(jnp.int32, (tm, LANES), 1)
    lane_f = lane.astype(F32)
    big = float(LANES)

    def first_argmax(vals, mx):
        return jnp.min(jnp.where(vals == mx, lane_f, big), axis=-1, keepdims=True)

    gl = jnp.where(lane < MOE_GROUPS, logits, -jnp.inf)
    gmax = jnp.max(gl, axis=-1, keepdims=True)
    grp_p = 1.0 / jnp.sum(jnp.exp(gl - gmax), axis=-1, keepdims=True)
    gidx = first_argmax(gl, gmax)
    lo = EXP_LANE0 + gidx * MOE_PER_GROUP
    in_grp = (lane_f >= lo) & (lane_f < lo + MOE_PER_GROUP)
    el = jnp.where(in_grp, logits, -jnp.inf)
    m1 = jnp.max(el, axis=-1, keepdims=True)
    i1 = first_argmax(el, m1)
    el2 = jnp.where(lane_f == i1, -jnp.inf, el)
    m2 = jnp.max(el2, axis=-1, keepdims=True)
    i2 = first_argmax(el2, m2)
    t2 = jnp.exp(m2 - m1)
    gate1 = grp_p / (1.0 + t2)
    gate2 = grp_p * t2 / (1.0 + t2)
    oh1 = lane_f == i1
    oh2 = lane_f == i2
    onehot = jnp.where(oh1 | oh2, 1.0, 0.0)
    ri = lax.broadcasted_iota(jnp.int32, (tm, tm), 0)
    ci = lax.broadcasted_iota(jnp.int32, (tm, tm), 1)
    before = jnp.where(ri > ci, 1.0, 0.0).astype(BF16)
    prior = _dot(before, onehot.astype(BF16)) + run_ref[0:1, :]
    rank1 = jnp.sum(jnp.where(oh1, prior, 0.0), axis=-1, keepdims=True)
    rank2 = jnp.sum(jnp.where(oh2, prior, 0.0), axis=-1, keepdims=True)
    run_new = run_ref[0:1, :] + jnp.sum(onehot, axis=0, keepdims=True)
    run_ref[0:1, :] = run_new
    cnt_ref[...] = jnp.broadcast_to(run_new, cnt_ref.shape)
    rec = jnp.zeros((tm, LANES), F32)
    for ln, val in ((META_E, i1 - EXP_LANE0), (META_E + 1, i2 - EXP_LANE0), (META_RANK, rank1),
                    (META_RANK + 1, rank2), (META_GATE, gate1), (META_GATE + 1, gate2)):
        rec = jnp.where(lane == ln, val, rec)
    meta_ref[...] = rec


def _router(x2d, g, w_group, w_expert):
    n, d = x2d.shape
    tm = min(TM_ROUTER, n)
    w = jnp.concatenate([w_group, w_expert], axis=1).astype(F32)
    w = jnp.pad(w, ((0, 0), (0, LANES - w.shape[1])))
    w_hi = w.astype(BF16)
    w_lo = (w - w_hi.astype(F32)).astype(BF16)
    return pl.pallas_call(
        _router_kernel,
        grid=(n // tm,),
        in_specs=[pl.BlockSpec((tm, d), lambda i: (i, 0)),
                  pl.BlockSpec((1, d), lambda i: (0, 0)),
                  pl.BlockSpec((d, LANES), lambda i: (0, 0)),
                  pl.BlockSpec((d, LANES), lambda i: (0, 0))],
        out_specs=[pl.BlockSpec((tm, LANES), lambda i: (i, 0)),
                   pl.BlockSpec((SUBLANES, LANES), lambda i: (0, 0))],
        out_shape=[jax.ShapeDtypeStruct((n, LANES), F32),
                   jax.ShapeDtypeStruct((SUBLANES, LANES), F32)],
        scratch_shapes=[pltpu.VMEM((SUBLANES, LANES), F32)],
        compiler_params=_cparams("arbitrary"),
        name="router",
    )(x2d, g.reshape(1, d), w_hi, w_lo)


DMA_UNROLL = 8


def _dispatch_kernel(dest_ref, x_ref, xs_init_hbm, xs_hbm, sem):
    del xs_init_hbm
    td = x_ref.shape[0]

    def row_copy(i, k):
        return pltpu.make_async_copy(x_ref.at[pl.ds(i, 1)], xs_hbm.at[pl.ds(dest_ref[2 * i + k], 1)], sem)

    def start(i, c):
        row_copy(i, 0).start()
        row_copy(i, 1).start()
        return c

    def wait(i, c):
        row_copy(i, 0).wait()
        row_copy(i, 1).wait()
        return c

    lax.fori_loop(0, td, start, 0, unroll=DMA_UNROLL)
    lax.fori_loop(0, td, wait, 0, unroll=DMA_UNROLL)


def _dispatch(x2d, dest, rows):
    n, d = x2d.shape
    td = min(TD_DISPATCH, n)
    return pl.pallas_call(
        _dispatch_kernel,
        grid=(n // td,),
        in_specs=[pl.BlockSpec((2 * td,), lambda i: (i,), memory_space=pltpu.SMEM),
                  pl.BlockSpec((td, d), lambda i: (i, 0)),
                  pl.BlockSpec(memory_space=pl.ANY)],
        out_specs=pl.BlockSpec(memory_space=pl.ANY),
        out_shape=jax.ShapeDtypeStruct((rows, d), F32),
        scratch_shapes=[pltpu.SemaphoreType.DMA],
        input_output_aliases={2: 0},
        compiler_params=_cparams("arbitrary"),
        name="moe_dispatch",
    )(dest, x2d, jnp.zeros((rows, d), F32))


def _experts_kernel(be_ref, nused_ref, xs_ref, g_ref, w1_ref, w3_ref, w2_ref, ys_ref):
    del be_ref
    i = pl.program_id(0)

    @pl.when(i < nused_ref[0])
    def _():
        x = xs_ref[...]
        h = (x * lax.rsqrt(jnp.mean(x * x, axis=-1, keepdims=True) + EPS) * g_ref[...]).astype(BF16)
        hid = _silu(_dot(h, w1_ref[0])) * _dot(h, w3_ref[0])
        ys_ref[...] = _dot(hid.astype(BF16), w2_ref[0])

    @pl.when(i >= nused_ref[0])
    def _():
        ys_ref[...] = jnp.zeros_like(ys_ref)


def _experts(xs, g, blk_expert, n_used, w1, w3, w2):
    rows, d = xs.shape
    nb = rows // BM_EXPERT
    ff = w1.shape[-1]
    grid_spec = pltpu.PrefetchScalarGridSpec(
        num_scalar_prefetch=2,
        grid=(nb,),
        in_specs=[pl.BlockSpec((BM_EXPERT, d), lambda i, be, nu: (i, 0)),
                  pl.BlockSpec((1, d), lambda i, be, nu: (0, 0)),
                  pl.BlockSpec((1, d, ff), lambda i, be, nu: (be[i], 0, 0)),
                  pl.BlockSpec((1, d, ff), lambda i, be, nu: (be[i], 0, 0)),
                  pl.BlockSpec((1, ff, d), lambda i, be, nu: (be[i], 0, 0))],
        out_specs=pl.BlockSpec((BM_EXPERT, d), lambda i, be, nu: (i, 0)),
    )
    return pl.pallas_call(
        _experts_kernel,
        grid_spec=grid_spec,
        out_shape=jax.ShapeDtypeStruct((rows, d), F32),
        compiler_params=_cparams("arbitrary"),
        name="moe_experts",
    )(blk_expert, n_used, xs, g.reshape(1, d), w1.astype(BF16), w3.astype(BF16), w2.astype(BF16))


def _combine_kernel(final, dest_ref, x_ref, meta_ref, ys_hbm, g_ref, o_ref, ybuf, sem):
    tc = x_ref.shape[0]

    def row_copy(i, k):
        return pltpu.make_async_copy(ys_hbm.at[pl.ds(dest_ref[2 * i + k], 1)], ybuf.at[k, pl.ds(i, 1)], sem)

    def start(i, c):
        row_copy(i, 0).start()
        row_copy(i, 1).start()
        return c

    def wait(i, c):
        row_copy(i, 0).wait()
        row_copy(i, 1).wait()
        return c

    lax.fori_loop(0, tc, start, 0, unroll=DMA_UNROLL)
    lax.fori_loop(0, tc, wait, 0, unroll=DMA_UNROLL)
    meta = meta_ref[...]
    y = (x_ref[...] + meta[:, META_GATE:META_GATE + 1] * ybuf[0]
         + meta[:, META_GATE + 1:META_GATE + 2] * ybuf[1])
    if final:
        y = y * lax.rsqrt(jnp.mean(y * y, axis=-1, keepdims=True) + EPS) * g_ref[...]
    o_ref[...] = y


def _combine(x2d, meta, dest, ys, final_g, final):
    n, d = x2d.shape
    tc = min(TC_COMBINE, n)
    return pl.pallas_call(
        functools.partial(_combine_kernel, final),
        grid=(n // tc,),
        in_specs=[pl.BlockSpec((2 * tc,), lambda i: (i,), memory_space=pltpu.SMEM),
                  pl.BlockSpec((tc, d), lambda i: (i, 0)),
                  pl.BlockSpec((tc, LANES), lambda i: (i, 0)),
                  pl.BlockSpec(memory_space=pl.ANY),
                  pl.BlockSpec((1, d), lambda i: (0, 0))],
        out_specs=pl.BlockSpec((tc, d), lambda i: (i, 0)),
        out_shape=jax.ShapeDtypeStruct((n, d), F32),
        scratch_shapes=[pltpu.VMEM((2, tc, d), F32), pltpu.SemaphoreType.DMA],
        compiler_params=_cparams("arbitrary"),
        name="moe_combine_final" if final else "moe_combine",
    )(dest, x2d, meta, ys, final_g.reshape(1, d))


def _moe(x2d, norm_g, w_group, w_expert, w1, w3, w2, final_g, final):
    n, d = x2d.shape
    meta, cnt = _router(x2d, norm_g, w_group, w_expert)
    counts = cnt[0, EXP_LANE0:EXP_LANE0 + MOE_EXPERTS].astype(jnp.int32)
    padded = (counts + BM_EXPERT - 1) // BM_EXPERT * BM_EXPERT
    pad_end = jnp.cumsum(padded)
    pad_start = pad_end - padded
    e = meta[:, META_E:META_E + 2].astype(jnp.int32)
    rank = meta[:, META_RANK:META_RANK + 2].astype(jnp.int32)
    eids = jnp.arange(MOE_EXPERTS, dtype=jnp.int32)
    dest = (jnp.sum(jnp.where(e[..., None] == eids, pad_start, 0), axis=-1) + rank).reshape(2 * n)
    rows = 2 * n + MOE_EXPERTS * BM_EXPERT
    nb = rows // BM_EXPERT
    blk_start = jnp.arange(nb, dtype=jnp.int32) * BM_EXPERT
    blk_expert = jnp.minimum(jnp.sum((pad_end[None, :] <= blk_start[:, None]).astype(jnp.int32), axis=1),
                             MOE_EXPERTS - 1)
    n_used = (pad_end[-1:] // BM_EXPERT).astype(jnp.int32)
    xs = _dispatch(x2d, dest, rows)
    ys = _experts(xs, norm_g, blk_expert, n_used, w1, w3, w2)
    return _combine(x2d, meta, dest, ys, final_g, final)


def _split_w_in(w):
    n_ab = 2 * A_WIDTH + 4 * B_WIDTH
    n_c = n_ab + 2 * B_HEADS
    n_g = n_c + 2 * 3 * C_WIDTH + C_WIDTH
    w_main = jnp.concatenate([w[:, n_g:], w[:, :n_ab]], axis=1).astype(BF16)
    w_c = w[:, n_c:n_g].astype(BF16)
    w_small = jnp.pad(w[:, n_ab:n_c], ((0, 0), (0, LANES - 2 * B_HEADS))).astype(BF16)
    return w_main, w_c, w_small


def kernel(x, norm_mix_g, w_in, a_ln_g, a_ln_b, a_w_s, a_b_s, b_conv_w, b_a_log, b_dt_bias, b_norm_g, rel_bias,
           w_branch, w_out, norm_ffn_g, w_group, w_expert, w1, w3, w2, final_norm_g):
    batch, seq, d = x.shape
    depth = w_in.shape[0]
    x2d = x.reshape(batch * seq, d)
    for l in range(depth):
        w_main, w_c, w_small = _split_w_in(w_in[l])
        assert w_main.shape[1] == PROJ_W and w_c.shape[1] == CPROJ_W
        proj = _rms_matmul(x2d, norm_mix_g[l], w_main, BF16, TN_PROJ)
        cproj = _rms_matmul(x2d, norm_mix_g[l], w_c, F32, TN_PROJ)
        p2 = _rms_matmul(x2d, norm_mix_g[l], w_small, F32, LANES)
        ya = _gmlp(proj, a_w_s[l], a_b_s[l], a_ln_g[l], a_ln_b[l])
        yb = _deltanet(proj, p2, b_conv_w[l], b_a_log[l], b_dt_bias[l], b_norm_g[l], batch, seq)
        yc = _dilated_attn(cproj, rel_bias, batch, seq)
        x2d = _merge(x2d, proj, ya, yb, yc, w_branch[l], w_out[l])
        x2d = _moe(x2d, norm_ffn_g[l], w_group[l], w_expert[l], w1[l], w3[l], w2[l], final_norm_g,
                   final=(l == depth - 1))
    return x2d.reshape(batch, seq, d)
```

```python
import functools
import math

import numpy as np
import jax
import jax.numpy as jnp
from jax import lax
from jax.experimental import pallas as pl
from jax.experimental.pallas import tpu as pltpu

F32 = jnp.float32
BF16 = jnp.bfloat16

D_MODEL = 1024
A_GROUPS, A_CHUNK, A_WIDTH = 4, 128, 512
B_HEADS, B_HEAD_DIM, B_CONV, B_WIDTH = 4, 128, 4, 512
C_PATTERNS = ((128, 1), (512, 4), (2048, 16))
C_HEADS, C_HEAD_DIM, C_BLOCK, C_WIDTH = 4, 128, 128, 512
REL_BUCKETS, REL_MAX_DIST = 32, 2048
MOE_GROUPS, MOE_PER_GROUP, MOE_EXPERTS, MOE_FF = 4, 8, 32, 512
EPS = 1e-6

LANES = 128
SUBLANES = 8
VMEM_LIMIT = 48 * 1024 * 1024

PROJ_W = 6144
GATE_OFF, AU_OFF, AV_OFF = 0, 3072, 3584
BQ_OFF, BK_OFF, BV_OFF, BZ_OFF = 4096, 4608, 5120, 5632
WBLK = 512
CPROJ_W = 3584
CQ_OFF, CK_OFF, CV_OFF = 0, 1536, 3072

TM_PROJ, TN_PROJ = 2048, 512
TM_GMLP = 512
TB_DELTA = 512
CHUNK = 128
TT_ATTN = 2048
TM_MERGE = 512
TM_ROUTER = 512
TD_DISPATCH = 512
BM_EXPERT = 256
TC_COMBINE = 256


def _cparams(*sem):
    return pltpu.CompilerParams(dimension_semantics=sem, vmem_limit_bytes=VMEM_LIMIT)


def _dot(a, b):
    return jnp.dot(a, b, preferred_element_type=F32)


def _dot_nt(a, b):
    return lax.dot_general(a, b, (((1,), (1,)), ((), ())), preferred_element_type=F32)


def _sigmoid(x):
    return 1.0 / (1.0 + jnp.exp(-x))


def _silu(x):
    return x * _sigmoid(x)


def _gelu_tanh(x):
    c = math.sqrt(2.0 / math.pi)
    return x * (0.5 * (1.0 + jnp.tanh(c * (x + 0.044715 * (x * x * x)))))


def _softplus(x):
    return jnp.maximum(x, 0.0) + jnp.log(1.0 + jnp.exp(-jnp.abs(x)))


def _rms_matmul_kernel(x_ref, g_ref, w_ref, o_ref, h_ref):
    @pl.when(pl.program_id(1) == 0)
    def _():
        x = x_ref[...]
        ms = jnp.mean(x * x, axis=-1, keepdims=True)
        h_ref[...] = (x * lax.rsqrt(ms + EPS) * g_ref[...]).astype(h_ref.dtype)

    o_ref[...] = _dot(h_ref[...], w_ref[...]).astype(o_ref.dtype)


def _rms_matmul(x2d, g, w, out_dtype, tn):
    n, d = x2d.shape
    ncols = w.shape[1]
    tm = min(TM_PROJ, n)
    return pl.pallas_call(
        _rms_matmul_kernel,
        grid=(n // tm, ncols // tn),
        in_specs=[pl.BlockSpec((tm, d), lambda i, j: (i, 0)),
                  pl.BlockSpec((1, d), lambda i, j: (0, 0)),
                  pl.BlockSpec((d, tn), lambda i, j: (0, j))],
        out_specs=pl.BlockSpec((tm, tn), lambda i, j: (i, j)),
        out_shape=jax.ShapeDtypeStruct((n, ncols), out_dtype),
        scratch_shapes=[pltpu.VMEM((tm, d), BF16)],
        compiler_params=_cparams("parallel", "arbitrary"),
        name="rms_matmul",
    )(x2d, g.reshape(1, d), w)


def _gmlp_kernel(u_ref, v_ref, w_ref, b_ref, lng_ref, lnb_ref, o_ref):
    tm = u_ref.shape[0]
    u = _gelu_tanh(u_ref[...].astype(F32))
    v = _gelu_tanh(v_ref[...].astype(F32))
    mu = jnp.mean(v, axis=-1, keepdims=True)
    vc = v - mu
    var = jnp.mean(vc * vc, axis=-1, keepdims=True)
    vn = (vc * lax.rsqrt(var + EPS) * lng_ref[...] + lnb_ref[...]).astype(BF16)
    for c in range(tm // A_CHUNK):
        rows = slice(c * A_CHUNK, (c + 1) * A_CHUNK)
        for g in range(A_GROUPS):
            cols = slice(g * LANES, (g + 1) * LANES)
            sv = _dot(w_ref[g], vn[rows, cols]) + b_ref[:, cols]
            o_ref[rows, cols] = (u[rows, cols] * sv).astype(o_ref.dtype)


def _gmlp(proj, w_s, b_s, ln_g, ln_b):
    n = proj.shape[0]
    tm = min(TM_GMLP, n)
    causal = np.tril(np.ones((A_CHUNK, A_CHUNK), dtype=bool))
    w = jnp.where(causal, w_s, 0.0).astype(BF16)
    bias = jnp.repeat(jnp.transpose(b_s), LANES, axis=1).astype(F32)
    return pl.pallas_call(
        _gmlp_kernel,
        grid=(n // tm,),
        in_specs=[pl.BlockSpec((tm, WBLK), lambda i: (i, AU_OFF // WBLK)),
                  pl.BlockSpec((tm, WBLK), lambda i: (i, AV_OFF // WBLK)),
                  pl.BlockSpec((A_GROUPS, A_CHUNK, A_CHUNK), lambda i: (0, 0, 0)),
                  pl.BlockSpec((A_CHUNK, A_WIDTH), lambda i: (0, 0)),
                  pl.BlockSpec((1, A_WIDTH), lambda i: (0, 0)),
                  pl.BlockSpec((1, A_WIDTH), lambda i: (0, 0))],
        out_specs=pl.BlockSpec((tm, A_WIDTH), lambda i: (i, 0)),
        out_shape=jax.ShapeDtypeStruct((n, A_WIDTH), BF16),
        compiler_params=_cparams("parallel"),
        name="gmlp",
    )(proj, proj, w, bias, ln_g.reshape(1, -1), ln_b.reshape(1, -1))


def _deltanet_kernel(q_ref, k_ref, v_ref, z_ref, p2_ref, cw_ref, alog_ref, dtb_ref, ng_ref, o_ref,
                     s_ref, carry_ref, qkv_ref, gc_ref, gct_ref, beta_ref, r_ref, p_ref, stack_ref, attn_ref,
                     rhs_ref, rhsb_ref, u_ref, wq_ref, kt_ref, eg_ref, vn_ref, op_ref):
    tb = q_ref.shape[0]
    nchunk = tb // CHUNK

    @pl.when(pl.program_id(1) == 0)
    def _():
        s_ref[...] = jnp.zeros_like(s_ref)
        carry_ref[...] = jnp.zeros_like(carry_ref)

    row8 = lax.broadcasted_iota(jnp.int32, (SUBLANES, B_WIDTH), 0)
    for idx, ref in enumerate((q_ref, k_ref, v_ref)):
        x = ref[...].astype(F32)
        w = cw_ref[:, idx * B_WIDTH:(idx + 1) * B_WIDTH]
        prev = carry_ref[idx]
        acc = x * w[B_CONV - 1:B_CONV]
        for s in range(1, B_CONV):
            xs = pltpu.roll(x, s, axis=0)
            ps = pltpu.roll(prev, s, axis=0)
            head = jnp.where(row8 < s, ps, xs[:SUBLANES])
            xs = jnp.concatenate([head, xs[SUBLANES:]], axis=0)
            acc = acc + xs * w[B_CONV - 1 - s:B_CONV - s]
        carry_ref[idx] = x[tb - SUBLANES:]
        qkv_ref[idx] = _silu(acc)

    ri = lax.broadcasted_iota(jnp.int32, (CHUNK, CHUNK), 0)
    ci = lax.broadcasted_iota(jnp.int32, (CHUNK, CHUNK), 1)
    incl = ri >= ci
    strict = ri > ci
    tril_f = incl.astype(F32)
    alog = alog_ref[...]
    dtb = dtb_ref[...]
    ng = ng_ref[...]
    lo, hi = slice(0, CHUNK), slice(CHUNK, 2 * CHUNK)
    items = [(c, h) for c in range(nchunk) for h in range(B_HEADS)]

    for c in range(nchunk):
        rows = slice(c * CHUNK, (c + 1) * CHUNK)
        p2 = p2_ref[rows, :]
        g_all = -jnp.exp(alog) * _softplus(p2 + dtb)
        gc_all = jnp.dot(tril_f, g_all, preferred_element_type=F32, precision=lax.Precision.HIGHEST)
        gc_ref[c] = gc_all
        gct_ref[c] = gc_all.T
        beta_ref[c] = _sigmoid(p2)

    for i, (c, h) in enumerate(items):
        rows = slice(c * CHUNK, (c + 1) * CHUNK)
        cols = slice(h * B_HEAD_DIM, (h + 1) * B_HEAD_DIM)
        q = qkv_ref[0, rows, cols]
        k = qkv_ref[1, rows, cols]
        v = qkv_ref[2, rows, cols]
        q = q * (lax.rsqrt(jnp.sum(q * q, axis=-1, keepdims=True) + EPS) * (B_HEAD_DIM ** -0.5))
        k = k * lax.rsqrt(jnp.sum(k * k, axis=-1, keepdims=True) + EPS)
        beta = beta_ref[c, :, h:h + 1]
        gc_col = jnp.broadcast_to(gc_ref[c, :, B_HEADS + h:B_HEADS + h + 1], (CHUNK, B_HEAD_DIM))
        gc_row = gct_ref[c, B_HEADS + h:B_HEADS + h + 1, :]
        decay = jnp.exp(jnp.where(incl, gc_col - gc_row, -jnp.inf))
        kb = k * beta
        vb = v * beta
        egc = jnp.exp(gc_col)
        aq = _dot_nt(jnp.concatenate([kb, q], axis=0).astype(BF16), k.astype(BF16))
        m = -jnp.where(strict, aq[lo] * decay, 0.0)
        r_ref[i] = m
        stack_ref[i, lo] = m.astype(BF16)
        attn_ref[i] = (aq[hi] * decay).astype(BF16)
        rhs = jnp.concatenate([vb, kb * egc], axis=1)
        rhs_ref[i] = rhs
        rhsb_ref[i] = rhs.astype(BF16)
        wq_ref[i, hi] = (q * egc).astype(BF16)
        g_last = gc_col[CHUNK - 1:CHUNK, :]
        kt_ref[i] = (k * jnp.exp(g_last - gc_col)).T.astype(BF16)
        eg_ref[i] = jnp.exp(g_last)

    for i in range(len(items)):
        mb = stack_ref[i, lo]
        p = _dot(mb, mb)
        p_ref[i] = p
        stack_ref[i, hi] = p.astype(BF16)
    for _ in range(5):
        for i in range(len(items)):
            rp = _dot(stack_ref[i], stack_ref[i, hi])
            r = r_ref[i] + p_ref[i] + rp[lo]
            p = rp[hi]
            r_ref[i] = r
            p_ref[i] = p
            stack_ref[i, lo] = r.astype(BF16)
            stack_ref[i, hi] = p.astype(BF16)
    for i in range(len(items)):
        r = r_ref[i] + p_ref[i] + _dot(stack_ref[i, lo], stack_ref[i, hi])
        sol = rhs_ref[i] + _dot(r.astype(BF16), rhsb_ref[i])
        u_ref[i] = sol[:, :B_HEAD_DIM]
        wq_ref[i, lo] = sol[:, B_HEAD_DIM:].astype(BF16)

    for c in range(nchunk):
        rows = slice(c * CHUNK, (c + 1) * CHUNK)
        for h in range(B_HEADS):
            i = c * B_HEADS + h
            wq = _dot(wq_ref[i], s_ref[h].astype(BF16))
            vn_ref[h] = (u_ref[i] - wq[lo]).astype(BF16)
            op_ref[h] = wq[hi]
        for h in range(B_HEADS):
            i = c * B_HEADS + h
            cols = slice(h * B_HEAD_DIM, (h + 1) * B_HEAD_DIM)
            vnb = vn_ref[h]
            o = op_ref[h] + _dot(attn_ref[i], vnb)
            s_ref[h] = s_ref[h] * eg_ref[i] + _dot(kt_ref[i], vnb)
            z = z_ref[rows, cols].astype(F32)
            on = o * lax.rsqrt(jnp.mean(o * o, axis=-1, keepdims=True) + EPS) * ng
            o_ref[rows, cols] = (on * _silu(z)).astype(o_ref.dtype)


def _deltanet(proj, p2, conv_w, a_log, dt_bias, norm_g, batch, seq):
    tb = min(TB_DELTA, seq)
    nt = seq // tb
    nchunk = tb // CHUNK
    ni = nchunk * B_HEADS
    dk = B_HEAD_DIM

    def col(off):
        return pl.BlockSpec((tb, WBLK), lambda b, t: (b * nt + t, off // WBLK))

    pad = LANES - 2 * B_HEADS
    alog = jnp.pad(a_log.astype(F32), (B_HEADS, pad)).reshape(1, LANES)
    dtb = jnp.pad(dt_bias.astype(F32), (B_HEADS, pad)).reshape(1, LANES)
    return pl.pallas_call(
        _deltanet_kernel,
        grid=(batch, nt),
        in_specs=[col(BQ_OFF), col(BK_OFF), col(BV_OFF), col(BZ_OFF),
                  pl.BlockSpec((tb, LANES), lambda b, t: (b * nt + t, 0)),
                  pl.BlockSpec((B_CONV, 3 * B_WIDTH), lambda b, t: (0, 0)),
                  pl.BlockSpec((1, LANES), lambda b, t: (0, 0)),
                  pl.BlockSpec((1, LANES), lambda b, t: (0, 0)),
                  pl.BlockSpec((1, B_HEAD_DIM), lambda b, t: (0, 0))],
        out_specs=pl.BlockSpec((tb, B_WIDTH), lambda b, t: (b * nt + t, 0)),
        out_shape=jax.ShapeDtypeStruct((batch * seq, B_WIDTH), BF16),
        scratch_shapes=[pltpu.VMEM((B_HEADS, dk, dk), F32),
                        pltpu.VMEM((3, SUBLANES, B_WIDTH), F32),
                        pltpu.VMEM((3, tb, B_WIDTH), F32),
                        pltpu.VMEM((nchunk, CHUNK, LANES), F32),
                        pltpu.VMEM((nchunk, LANES, CHUNK), F32),
                        pltpu.VMEM((nchunk, CHUNK, LANES), F32),
                        pltpu.VMEM((ni, CHUNK, CHUNK), F32),
                        pltpu.VMEM((ni, CHUNK, CHUNK), F32),
                        pltpu.VMEM((ni, 2 * CHUNK, CHUNK), BF16),
                        pltpu.VMEM((ni, CHUNK, CHUNK), BF16),
                        pltpu.VMEM((ni, CHUNK, 2 * dk), F32),
                        pltpu.VMEM((ni, CHUNK, 2 * dk), BF16),
                        pltpu.VMEM((ni, CHUNK, dk), F32),
                        pltpu.VMEM((ni, 2 * CHUNK, dk), BF16),
                        pltpu.VMEM((ni, dk, CHUNK), BF16),
                        pltpu.VMEM((ni, 1, dk), F32),
                        pltpu.VMEM((B_HEADS, CHUNK, dk), BF16),
                        pltpu.VMEM((B_HEADS, CHUNK, dk), F32)],
        compiler_params=_cparams("parallel", "arbitrary"),
        name="deltanet",
    )(proj, proj, proj, proj, p2, conv_w.astype(F32), alog, dtb, norm_g.reshape(1, -1).astype(F32))


def _t5_bucket(dist):
    max_exact = REL_BUCKETS // 2
    n = np.maximum(dist, 0)
    large = max_exact + (np.log(np.maximum(n, 1) / max_exact) / math.log(REL_MAX_DIST / max_exact)
                         * (REL_BUCKETS - max_exact)).astype(np.int32)
    large = np.minimum(large, REL_BUCKETS - 1)
    return np.where(n < max_exact, n, large).astype(np.int32)


def _band_pattern(window, dil):
    i = np.arange(C_BLOCK)[:, None]
    kk = np.arange(2 * C_BLOCK)[None, :]
    j = C_BLOCK + i - kk
    band = (j >= 0) & (j <= window // dil)
    bucket = _t5_bucket(np.clip(j, 0, None) * dil)
    return band, bucket


def _attn_kernel(q0_ref, q1_ref, q2_ref, k0_ref, k1_ref, k2_ref, kp0_ref, kp1_ref, kp2_ref, v_ref, vp_ref,
                 bias_ref, o_ref, s_ref, p_ref, inv_ref, og_ref, lse_ref):
    tt = q0_ref.shape[0]
    first = pl.program_id(1) == 0
    q_refs = (q0_ref, q1_ref, q2_ref)
    k_refs = (k0_ref, k1_ref, k2_ref)
    kp_refs = (kp0_ref, kp1_ref, kp2_ref)
    scale = C_HEAD_DIM ** -0.5
    shape = (C_BLOCK, C_HEAD_DIM)
    lo, hi = slice(0, C_BLOCK), slice(C_BLOCK, 2 * C_BLOCK)

    for g, (_, dil) in enumerate(C_PATTERNS):
        q_ref, k_ref, kp_ref = q_refs[g], k_refs[g], kp_refs[g]
        span = C_BLOCK * dil
        nblk = tt // span
        bias_p = bias_ref[g, 0, :, lo]
        bias_c = bias_ref[g, 0, :, hi]

        def rows(start, dil=dil):
            return pl.ds(start, C_BLOCK, stride=dil) if dil > 1 else pl.ds(start, C_BLOCK)

        blocks = [(rows(r + n * span), rows(r + (n - 1) * span) if n else rows(r + (nblk - 1) * span), n == 0)
                  for r in range(dil) for n in range(nblk)]

        for j, (sl, psl, from_prev_tile) in enumerate(blocks):
            q = (q_ref[sl, :] * scale).astype(BF16)
            kp = (kp_ref if from_prev_tile else k_ref)[psl, :].astype(BF16)
            s_p = _dot_nt(q, kp) + bias_p
            if from_prev_tile:
                s_p = jnp.where(first, -jnp.inf, s_p)
            s_ref[j, :, lo] = s_p
            s_ref[j, :, hi] = _dot_nt(q, k_ref[sl, :].astype(BF16)) + bias_c

        for j, (sl, psl, from_prev_tile) in enumerate(blocks):
            s = s_ref[j]
            mx = jnp.max(s, axis=-1, keepdims=True)
            p = jnp.exp(s - mx)
            den = jnp.sum(p, axis=-1, keepdims=True)
            p_ref[j] = p.astype(BF16)
            inv_ref[j] = jnp.broadcast_to(1.0 / den, shape)
            lse_ref[g, sl, :] = jnp.broadcast_to(mx + jnp.log(den), shape)

        for j, (sl, psl, from_prev_tile) in enumerate(blocks):
            vp = (vp_ref if from_prev_tile else v_ref)[psl, :].astype(BF16)
            num = _dot(p_ref[j, :, lo], vp) + _dot(p_ref[j, :, hi], v_ref[sl, :].astype(BF16))
            og_ref[g, sl, :] = num * inv_ref[j]

    piece = 2 * C_BLOCK
    for i in range(tt // piece):
        sl = slice(i * piece, (i + 1) * piece)
        l0, l1, l2 = lse_ref[0, sl, :], lse_ref[1, sl, :], lse_ref[2, sl, :]
        lm = jnp.maximum(jnp.maximum(l0, l1), l2)
        e0, e1, e2 = jnp.exp(l0 - lm), jnp.exp(l1 - lm), jnp.exp(l2 - lm)
        num = e0 * og_ref[0, sl, :] + e1 * og_ref[1, sl, :] + e2 * og_ref[2, sl, :]
        o_ref[sl, :] = (num / (e0 + e1 + e2)).astype(o_ref.dtype)


def _dilated_attn(cproj, rel_bias, batch, seq):
    tt = TT_ATTN
    assert seq % tt == 0 and tt == C_BLOCK * max(d for _, d in C_PATTERNS)
    nt = seq // tt
    biases = []
    for gi, (window, dil) in enumerate(C_PATTERNS):
        band, bucket = _band_pattern(window, dil)
        onehot = jnp.asarray(bucket[..., None] == np.arange(REL_BUCKETS), F32)
        table = rel_bias[:, gi * C_HEADS:(gi + 1) * C_HEADS].astype(F32)
        bias = jnp.einsum('qkn,nh->hqk', onehot, table, precision=lax.Precision.HIGHEST)
        biases.append(jnp.where(band[None], bias, -jnp.inf))
    bias = jnp.stack(biases)

    def cur(off, g=0):
        c0 = off // C_HEAD_DIM + g * C_HEADS
        return pl.BlockSpec((tt, C_HEAD_DIM), lambda b, t, h: (b * nt + t, c0 + h))

    def prev(off, g=0):
        c0 = off // C_HEAD_DIM + g * C_HEADS
        return pl.BlockSpec((tt, C_HEAD_DIM), lambda b, t, h: (b * nt + jnp.maximum(t - 1, 0), c0 + h))

    ng = len(C_PATTERNS)
    nblocks = tt // C_BLOCK
    in_specs = ([cur(CQ_OFF, g) for g in range(ng)] + [cur(CK_OFF, g) for g in range(ng)]
                + [prev(CK_OFF, g) for g in range(ng)] + [cur(CV_OFF), prev(CV_OFF)]
                + [pl.BlockSpec((ng, 1, C_BLOCK, 2 * C_BLOCK), lambda b, t, h: (0, h, 0, 0))])
    return pl.pallas_call(
        _attn_kernel,
        grid=(batch, nt, C_HEADS),
        in_specs=in_specs,
        out_specs=pl.BlockSpec((tt, C_HEAD_DIM), lambda b, t, h: (b * nt + t, h)),
        out_shape=jax.ShapeDtypeStruct((batch * seq, C_WIDTH), BF16),
        scratch_shapes=[pltpu.VMEM((nblocks, C_BLOCK, 2 * C_BLOCK), F32),
                        pltpu.VMEM((nblocks, C_BLOCK, 2 * C_BLOCK), BF16),
                        pltpu.VMEM((nblocks, C_BLOCK, C_HEAD_DIM), F32),
                        pltpu.VMEM((ng, tt, C_HEAD_DIM), F32),
                        pltpu.VMEM((ng, tt, C_HEAD_DIM), F32)],
        compiler_params=_cparams("parallel", "arbitrary", "arbitrary"),
        name="dilated_attn",
    )(*([cproj] * (3 * ng + 2)), bias)


def _merge_kernel(x_ref, g0_ref, g1_ref, g2_ref, ya_ref, yb_ref, yc_ref, wb_ref, wo_ref, out_ref):
    merged = (_sigmoid(g0_ref[...].astype(F32)) * _dot(ya_ref[...], wb_ref[0])
              + _sigmoid(g1_ref[...].astype(F32)) * _dot(yb_ref[...], wb_ref[1])
              + _sigmoid(g2_ref[...].astype(F32)) * _dot(yc_ref[...], wb_ref[2]))
    out_ref[...] = x_ref[...] + _dot(merged.astype(BF16), wo_ref[...])


def _merge(x2d, proj, ya, yb, yc, w_branch, w_out):
    n, d = x2d.shape
    tm = min(TM_MERGE, n)
    row = lambda w: pl.BlockSpec((tm, w), lambda i: (i, 0))
    gate = lambda g: pl.BlockSpec((tm, d), lambda i: (i, GATE_OFF // d + g))
    return pl.pallas_call(
        _merge_kernel,
        grid=(n // tm,),
        in_specs=[row(d), gate(0), gate(1), gate(2), row(WBLK), row(WBLK), row(WBLK),
                  pl.BlockSpec((3, WBLK, d), lambda i: (0, 0, 0)),
                  pl.BlockSpec((d, d), lambda i: (0, 0))],
        out_specs=row(d),
        out_shape=jax.ShapeDtypeStruct((n, d), F32),
        compiler_params=_cparams("parallel"),
        name="merge",
    )(x2d, proj, proj, proj, ya, yb, yc, w_branch.astype(BF16), w_out.astype(BF16))


EXP_LANE0 = MOE_GROUPS
META_E, META_RANK, META_GATE = 0, 2, 4


def _split3_dot(h, w_hi, w_lo):
    h_hi = h.astype(BF16)
    h_lo = (h - h_hi.astype(F32)).astype(BF16)
    return _dot(h_hi, w_hi) + (_dot(h_lo, w_hi) + _dot(h_hi, w_lo))


def _router_kernel(x_ref, g_ref, whi_ref, wlo_ref, meta_ref, cnt_ref, run_ref):
    tm = x_ref.shape[0]

    @pl.when(pl.program_id(0) == 0)
    def _():
        run_ref[...] = jnp.zeros_like(run_ref)

    x = x_ref[...]
    h = x * lax.rsqrt(jnp.mean(x * x, axis=-1, keepdims=True) + EPS) * g_ref[...]
    logits = _split3_dot(h, whi_ref[...], wlo_ref[...])
    lane = lax.broadcasted_iota(jnp.int32, (tm, LANES), 1)
    lane_f = lane.astype(F32)
    big = float(LANES)

    def first_argmax(vals, mx):
        return jnp.min(jnp.where(vals == mx, lane_f, big), axis=-1, keepdims=True)

    gl = jnp.where(lane < MOE_GROUPS, logits, -jnp.inf)
    gmax = jnp.max(gl, axis=-1, keepdims=True)
    grp_p = 1.0 / jnp.sum(jnp.exp(gl - gmax), axis=-1, keepdims=True)
    gidx = first_argmax(gl, gmax)
    lo = EXP_LANE0 + gidx * MOE_PER_GROUP
    in_grp = (lane_f >= lo) & (lane_f < lo + MOE_PER_GROUP)
    el = jnp.where(in_grp, logits, -jnp.inf)
    m1 = jnp.max(el, axis=-1, keepdims=True)
    i1 = first_argmax(el, m1)
    el2 = jnp.where(lane_f == i1, -jnp.inf, el)
    m2 = jnp.max(el2, axis=-1, keepdims=True)
    i2 = first_argmax(el2, m2)
    t2 = jnp.exp(m2 - m1)
    gate1 = grp_p / (1.0 + t2)
    gate2 = grp_p * t2 / (1.0 + t2)
    oh1 = lane_f == i1
    oh2 = lane_f == i2
    onehot = jnp.where(oh1 | oh2, 1.0, 0.0)
    ri = lax.broadcasted_iota(jnp.int32, (tm, tm), 0)
    ci = lax.broadcasted_iota(jnp.int32, (tm, tm), 1)
    before = jnp.where(ri > ci, 1.0, 0.0).astype(BF16)
    prior = _dot(before, onehot.astype(BF16)) + run_ref[0:1, :]
    rank1 = jnp.sum(jnp.where(oh1, prior, 0.0), axis=-1, keepdims=True)
    rank2 = jnp.sum(jnp.where(oh2, prior, 0.0), axis=-1, keepdims=True)
    run_new = run_ref[0:1, :] + jnp.sum(onehot, axis=0, keepdims=True)
    run_ref[0:1, :] = run_new
    cnt_ref[...] = jnp.broadcast_to(run_new, cnt_ref.shape)
    rec = jnp.zeros((tm, LANES), F32)
    for ln, val in ((META_E, i1 - EXP_LANE0), (META_E + 1, i2 - EXP_LANE0), (META_RANK, rank1),
                    (META_RANK + 1, rank2), (META_GATE, gate1), (META_GATE + 1, gate2)):
        rec = jnp.where(lane == ln, val, rec)
    meta_ref[...] = rec


def _router(x2d, g, w_group, w_expert):
    n, d = x2d.shape
    tm = min(TM_ROUTER, n)
    w = jnp.concatenate([w_group, w_expert], axis=1).astype(F32)
    w = jnp.pad(w, ((0, 0), (0, LANES - w.shape[1])))
    w_hi = w.astype(BF16)
    w_lo = (w - w_hi.astype(F32)).astype(BF16)
    return pl.pallas_call(
        _router_kernel,
        grid=(n // tm,),
        in_specs=[pl.BlockSpec((tm, d), lambda i: (i, 0)),
                  pl.BlockSpec((1, d), lambda i: (0, 0)),
                  pl.BlockSpec((d, LANES), lambda i: (0, 0)),
                  pl.BlockSpec((d, LANES), lambda i: (0, 0))],
        out_specs=[pl.BlockSpec((tm, LANES), lambda i: (i, 0)),
                   pl.BlockSpec((SUBLANES, LANES), lambda i: (0, 0))],
        out_shape=[jax.ShapeDtypeStruct((n, LANES), F32),
                   jax.ShapeDtypeStruct((SUBLANES, LANES), F32)],
        scratch_shapes=[pltpu.VMEM((SUBLANES, LANES), F32)],
        compiler_params=_cparams("arbitrary"),
        name="router",
    )(x2d, g.reshape(1, d), w_hi, w_lo)


DMA_UNROLL = 8


def _dispatch_kernel(dest_ref, x_ref, xs_init_hbm, xs_hbm, sem):
    del xs_init_hbm
    td = x_ref.shape[0]

    def row_copy(i, k):
        return pltpu.make_async_copy(x_ref.at[pl.ds(i, 1)], xs_hbm.at[pl.ds(dest_ref[2 * i + k], 1)], sem)

    def start(i, c):
        row_copy(i, 0).start()
        row_copy(i, 1).start()
        return c

    def wait(i, c):
        row_copy(i, 0).wait()
        row_copy(i, 1).wait()
        return c

    lax.fori_loop(0, td, start, 0, unroll=DMA_UNROLL)
    lax.fori_loop(0, td, wait, 0, unroll=DMA_UNROLL)


def _dispatch(x2d, dest, rows):
    n, d = x2d.shape
    td = min(TD_DISPATCH, n)
    return pl.pallas_call(
        _dispatch_kernel,
        grid=(n // td,),
        in_specs=[pl.BlockSpec((2 * td,), lambda i: (i,), memory_space=pltpu.SMEM),
                  pl.BlockSpec((td, d), lambda i: (i, 0)),
                  pl.BlockSpec(memory_space=pl.ANY)],
        out_specs=pl.BlockSpec(memory_space=pl.ANY),
        out_shape=jax.ShapeDtypeStruct((rows, d), F32),
        scratch_shapes=[pltpu.SemaphoreType.DMA],
        input_output_aliases={2: 0},
        compiler_params=_cparams("arbitrary"),
        name="moe_dispatch",
    )(dest, x2d, jnp.zeros((rows, d), F32))


def _experts_kernel(be_ref, nused_ref, xs_ref, g_ref, w1_ref, w3_ref, w2_ref, ys_ref):
    del be_ref
    i = pl.program_id(0)

    @pl.when(i < nused_ref[0])
    def _():
        x = xs_ref[...]
        h = (x * lax.rsqrt(jnp.mean(x * x, axis=-1, keepdims=True) + EPS) * g_ref[...]).astype(BF16)
        hid = _silu(_dot(h, w1_ref[0])) * _dot(h, w3_ref[0])
        ys_ref[...] = _dot(hid.astype(BF16), w2_ref[0])

    @pl.when(i >= nused_ref[0])
    def _():
        ys_ref[...] = jnp.zeros_like(ys_ref)


def _experts(xs, g, blk_expert, n_used, w1, w3, w2):
    rows, d = xs.shape
    nb = rows // BM_EXPERT
    ff = w1.shape[-1]
    grid_spec = pltpu.PrefetchScalarGridSpec(
        num_scalar_prefetch=2,
        grid=(nb,),
        in_specs=[pl.BlockSpec((BM_EXPERT, d), lambda i, be, nu: (i, 0)),
                  pl.BlockSpec((1, d), lambda i, be, nu: (0, 0)),
                  pl.BlockSpec((1, d, ff), lambda i, be, nu: (be[i], 0, 0)),
                  pl.BlockSpec((1, d, ff), lambda i, be, nu: (be[i], 0, 0)),
                  pl.BlockSpec((1, ff, d), lambda i, be, nu: (be[i], 0, 0))],
        out_specs=pl.BlockSpec((BM_EXPERT, d), lambda i, be, nu: (i, 0)),
    )
    return pl.pallas_call(
        _experts_kernel,
        grid_spec=grid_spec,
        out_shape=jax.ShapeDtypeStruct((rows, d), F32),
        compiler_params=_cparams("arbitrary"),
        name="moe_experts",
    )(blk_expert, n_used, xs, g.reshape(1, d), w1.astype(BF16), w3.astype(BF16), w2.astype(BF16))


def _combine_kernel(final, dest_ref, x_ref, meta_ref, ys_hbm, g_ref, o_ref, ybuf, sem):
    tc = x_ref.shape[0]

    def row_copy(i, k):
        return pltpu.make_async_copy(ys_hbm.at[pl.ds(dest_ref[2 * i + k], 1)], ybuf.at[k, pl.ds(i, 1)], sem)

    def start(i, c):
        row_copy(i, 0).start()
        row_copy(i, 1).start()
        return c

    def wait(i, c):
        row_copy(i, 0).wait()
        row_copy(i, 1).wait()
        return c

    lax.fori_loop(0, tc, start, 0, unroll=DMA_UNROLL)
    lax.fori_loop(0, tc, wait, 0, unroll=DMA_UNROLL)
    meta = meta_ref[...]
    y = (x_ref[...] + meta[:, META_GATE:META_GATE + 1] * ybuf[0]
         + meta[:, META_GATE + 1:META_GATE + 2] * ybuf[1])
    if final:
        y = y * lax.rsqrt(jnp.mean(y * y, axis=-1, keepdims=True) + EPS) * g_ref[...]
    o_ref[...] = y


def _combine(x2d, meta, dest, ys, final_g, final):
    n, d = x2d.shape
    tc = min(TC_COMBINE, n)
    return pl.pallas_call(
        functools.partial(_combine_kernel, final),
        grid=(n // tc,),
        in_specs=[pl.BlockSpec((2 * tc,), lambda i: (i,), memory_space=pltpu.SMEM),
                  pl.BlockSpec((tc, d), lambda i: (i, 0)),
                  pl.BlockSpec((tc, LANES), lambda i: (i, 0)),
                  pl.BlockSpec(memory_space=pl.ANY),
                  pl.BlockSpec((1, d), lambda i: (0, 0))],
        out_specs=pl.BlockSpec((tc, d), lambda i: (i, 0)),
        out_shape=jax.ShapeDtypeStruct((n, d), F32),
        scratch_shapes=[pltpu.VMEM((2, tc, d), F32), pltpu.SemaphoreType.DMA],
        compiler_params=_cparams("arbitrary"),
        name="moe_combine_final" if final else "moe_combine",
    )(dest, x2d, meta, ys, final_g.reshape(1, d))


def _moe(x2d, norm_g, w_group, w_expert, w1, w3, w2, final_g, final):
    n, d = x2d.shape
    meta, cnt = _router(x2d, norm_g, w_group, w_expert)
    counts = cnt[0, EXP_LANE0:EXP_LANE0 + MOE_EXPERTS].astype(jnp.int32)
    padded = (counts + BM_EXPERT - 1) // BM_EXPERT * BM_EXPERT
    pad_end = jnp.cumsum(padded)
    pad_start = pad_end - padded
    e = meta[:, META_E:META_E + 2].astype(jnp.int32)
    rank = meta[:, META_RANK:META_RANK + 2].astype(jnp.int32)
    eids = jnp.arange(MOE_EXPERTS, dtype=jnp.int32)
    dest = (jnp.sum(jnp.where(e[..., None] == eids, pad_start, 0), axis=-1) + rank).reshape(2 * n)
    rows = 2 * n + MOE_EXPERTS * BM_EXPERT
    nb = rows // BM_EXPERT
    blk_start = jnp.arange(nb, dtype=jnp.int32) * BM_EXPERT
    blk_expert = jnp.minimum(jnp.sum((pad_end[None, :] <= blk_start[:, None]).astype(jnp.int32), axis=1),
                             MOE_EXPERTS - 1)
    n_used = (pad_end[-1:] // BM_EXPERT).astype(jnp.int32)
    xs = _dispatch(x2d, dest, rows)
    ys = _experts(xs, norm_g, blk_expert, n_used, w1, w3, w2)
    return _combine(x2d, meta, dest, ys, final_g, final)


def _split_w_in(w):
    n_ab = 2 * A_WIDTH + 4 * B_WIDTH
    n_c = n_ab + 2 * B_HEADS
    n_g = n_c + 2 * 3 * C_WIDTH + C_WIDTH
    w_main = jnp.concatenate([w[:, n_g:], w[:, :n_ab]], axis=1).astype(BF16)
    w_c = w[:, n_c:n_g].astype(BF16)
    w_small = jnp.pad(w[:, n_ab:n_c], ((0, 0), (0, LANES - 2 * B_HEADS))).astype(BF16)
    return w_main, w_c, w_small


def kernel(x, norm_mix_g, w_in, a_ln_g, a_ln_b, a_w_s, a_b_s, b_conv_w, b_a_log, b_dt_bias, b_norm_g, rel_bias,
           w_branch, w_out, norm_ffn_g, w_group, w_expert, w1, w3, w2, final_norm_g):
    batch, seq, d = x.shape
    depth = w_in.shape[0]
    x2d = x.reshape(batch * seq, d)
    for l in range(depth):
        w_main, w_c, w_small = _split_w_in(w_in[l])
        assert w_main.shape[1] == PROJ_W and w_c.shape[1] == CPROJ_W
        proj = _rms_matmul(x2d, norm_mix_g[l], w_main, BF16, TN_PROJ)
        cproj = _rms_matmul(x2d, norm_mix_g[l], w_c, F32, TN_PROJ)
        p2 = _rms_matmul(x2d, norm_mix_g[l], w_small, F32, LANES)
        ya = _gmlp(proj, a_w_s[l], a_b_s[l], a_ln_g[l], a_ln_b[l])
        yb = _deltanet(proj, p2, b_conv_w[l], b_a_log[l], b_dt_bias[l], b_norm_g[l], batch, seq)
        yc = _dilated_attn(cproj, rel_bias, batch, seq)
        x2d = _merge(x2d, proj, ya, yb, yc, w_branch[l], w_out[l])
        x2d = _moe(x2d, norm_ffn_g[l], w_group[l], w_expert[l], w1[l], w3[l], w2[l], final_norm_g,
                   final=(l == depth - 1))
    return x2d.reshape(batch, seq, d)
```

```python
import functools
import math

import numpy as np
import jax
import jax.numpy as jnp
from jax import lax
from jax.experimental import pallas as pl
from jax.experimental.pallas import tpu as pltpu

F32 = jnp.float32
BF16 = jnp.bfloat16

D_MODEL = 1024
A_GROUPS, A_CHUNK, A_WIDTH = 4, 128, 512
B_HEADS, B_HEAD_DIM, B_CONV, B_WIDTH = 4, 128, 4, 512
C_PATTERNS = ((128, 1), (512, 4), (2048, 16))
C_HEADS, C_HEAD_DIM, C_BLOCK, C_WIDTH = 4, 128, 128, 512
REL_BUCKETS, REL_MAX_DIST = 32, 2048
MOE_GROUPS, MOE_PER_GROUP, MOE_EXPERTS, MOE_FF = 4, 8, 32, 512
EPS = 1e-6

LANES = 128
SUBLANES = 8
VMEM_LIMIT = 48 * 1024 * 1024

PROJ_W = 6144
GATE_OFF, AU_OFF, AV_OFF = 0, 3072, 3584
BQ_OFF, BK_OFF, BV_OFF, BZ_OFF = 4096, 4608, 5120, 5632
WBLK = 512
CPROJ_W = 3584
CQ_OFF, CK_OFF, CV_OFF = 0, 1536, 3072

TM_PROJ, TN_PROJ, TN_PROJ_MAIN = 2048, 512, 1024
TM_GMLP = 512
TB_DELTA = 512
CHUNK = 128
TT_ATTN = 2048
TM_MERGE = 512
TM_ROUTER = 512
TD_DISPATCH = 512
BM_EXPERT = 256
TC_COMBINE = 512


def _cparams(*sem):
    return pltpu.CompilerParams(dimension_semantics=sem, vmem_limit_bytes=VMEM_LIMIT)


def _dot(a, b):
    return jnp.dot(a, b, preferred_element_type=F32)


def _dot_nt(a, b):
    return lax.dot_general(a, b, (((1,), (1,)), ((), ())), preferred_element_type=F32)


def _sigmoid(x):
    return 1.0 / (1.0 + jnp.exp(-x))


def _silu(x):
    return x * _sigmoid(x)


def _gelu_tanh(x):
    c = math.sqrt(2.0 / math.pi)
    return x * (0.5 * (1.0 + jnp.tanh(c * (x + 0.044715 * (x * x * x)))))


def _softplus(x):
    return jnp.maximum(x, 0.0) + jnp.log(1.0 + jnp.exp(-jnp.abs(x)))


def _rms_matmul_kernel(x_ref, g_ref, w_ref, o_ref, h_ref):
    @pl.when(pl.program_id(1) == 0)
    def _():
        x = x_ref[...]
        ms = jnp.mean(x * x, axis=-1, keepdims=True)
        h_ref[...] = (x * lax.rsqrt(ms + EPS) * g_ref[...]).astype(h_ref.dtype)

    o_ref[...] = _dot(h_ref[...], w_ref[...]).astype(o_ref.dtype)


def _rms_matmul(x2d, g, w, out_dtype, tn):
    n, d = x2d.shape
    ncols = w.shape[1]
    tm = min(TM_PROJ, n)
    return pl.pallas_call(
        _rms_matmul_kernel,
        grid=(n // tm, ncols // tn),
        in_specs=[pl.BlockSpec((tm, d), lambda i, j: (i, 0)),
                  pl.BlockSpec((1, d), lambda i, j: (0, 0)),
                  pl.BlockSpec((d, tn), lambda i, j: (0, j))],
        out_specs=pl.BlockSpec((tm, tn), lambda i, j: (i, j)),
        out_shape=jax.ShapeDtypeStruct((n, ncols), out_dtype),
        scratch_shapes=[pltpu.VMEM((tm, d), BF16)],
        compiler_params=_cparams("parallel", "arbitrary"),
        name="rms_matmul",
    )(x2d, g.reshape(1, d), w)


def _gmlp_kernel(u_ref, v_ref, w_ref, b_ref, lng_ref, lnb_ref, o_ref):
    tm = u_ref.shape[0]
    u = _gelu_tanh(u_ref[...].astype(F32))
    v = _gelu_tanh(v_ref[...].astype(F32))
    mu = jnp.mean(v, axis=-1, keepdims=True)
    vc = v - mu
    var = jnp.mean(vc * vc, axis=-1, keepdims=True)
    vn = (vc * lax.rsqrt(var + EPS) * lng_ref[...] + lnb_ref[...]).astype(BF16)
    for c in range(tm // A_CHUNK):
        rows = slice(c * A_CHUNK, (c + 1) * A_CHUNK)
        for g in range(A_GROUPS):
            cols = slice(g * LANES, (g + 1) * LANES)
            sv = _dot(w_ref[g], vn[rows, cols]) + b_ref[:, cols]
            o_ref[rows, cols] = (u[rows, cols] * sv).astype(o_ref.dtype)


def _gmlp(proj, w_s, b_s, ln_g, ln_b):
    n = proj.shape[0]
    tm = min(TM_GMLP, n)
    causal = np.tril(np.ones((A_CHUNK, A_CHUNK), dtype=bool))
    w = jnp.where(causal, w_s, 0.0).astype(BF16)
    bias = jnp.repeat(jnp.transpose(b_s), LANES, axis=1).astype(F32)
    return pl.pallas_call(
        _gmlp_kernel,
        grid=(n // tm,),
        in_specs=[pl.BlockSpec((tm, WBLK), lambda i: (i, AU_OFF // WBLK)),
                  pl.BlockSpec((tm, WBLK), lambda i: (i, AV_OFF // WBLK)),
                  pl.BlockSpec((A_GROUPS, A_CHUNK, A_CHUNK), lambda i: (0, 0, 0)),
                  pl.BlockSpec((A_CHUNK, A_WIDTH), lambda i: (0, 0)),
                  pl.BlockSpec((1, A_WIDTH), lambda i: (0, 0)),
                  pl.BlockSpec((1, A_WIDTH), lambda i: (0, 0))],
        out_specs=pl.BlockSpec((tm, A_WIDTH), lambda i: (i, 0)),
        out_shape=jax.ShapeDtypeStruct((n, A_WIDTH), BF16),
        compiler_params=_cparams("parallel"),
        name="gmlp",
    )(proj, proj, w, bias, ln_g.reshape(1, -1), ln_b.reshape(1, -1))


def _deltanet_kernel(q_ref, k_ref, v_ref, z_ref, p2_ref, cw_ref, alog_ref, dtb_ref, ng_ref, o_ref,
                     s_ref, carry_ref, qkv_ref, gc_ref, gct_ref, beta_ref, r_ref, p_ref, stack_ref, attn_ref,
                     rhs_ref, rhsb_ref, u_ref, wq_ref, kt_ref, eg_ref, vn_ref, op_ref):
    tb = q_ref.shape[0]
    nchunk = tb // CHUNK

    @pl.when(pl.program_id(1) == 0)
    def _():
        s_ref[...] = jnp.zeros_like(s_ref)
        carry_ref[...] = jnp.zeros_like(carry_ref)

    row8 = lax.broadcasted_iota(jnp.int32, (SUBLANES, B_WIDTH), 0)
    for idx, ref in enumerate((q_ref, k_ref, v_ref)):
        x = ref[...].astype(F32)
        w = cw_ref[:, idx * B_WIDTH:(idx + 1) * B_WIDTH]
        prev = carry_ref[idx]
        acc = x * w[B_CONV - 1:B_CONV]
        for s in range(1, B_CONV):
            xs = pltpu.roll(x, s, axis=0)
            ps = pltpu.roll(prev, s, axis=0)
            head = jnp.where(row8 < s, ps, xs[:SUBLANES])
            xs = jnp.concatenate([head, xs[SUBLANES:]], axis=0)
            acc = acc + xs * w[B_CONV - 1 - s:B_CONV - s]
        carry_ref[idx] = x[tb - SUBLANES:]
        qkv_ref[idx] = _silu(acc)

    ri = lax.broadcasted_iota(jnp.int32, (CHUNK, CHUNK), 0)
    ci = lax.broadcasted_iota(jnp.int32, (CHUNK, CHUNK), 1)
    incl = ri >= ci
    strict = ri > ci
    tril_f = incl.astype(F32)
    alog = alog_ref[...]
    dtb = dtb_ref[...]
    ng = ng_ref[...]
    lo, hi = slice(0, CHUNK), slice(CHUNK, 2 * CHUNK)
    items = [(c, h) for c in range(nchunk) for h in range(B_HEADS)]

    for c in range(nchunk):
        rows = slice(c * CHUNK, (c + 1) * CHUNK)
        p2 = p2_ref[rows, :]
        g_all = -jnp.exp(alog) * _softplus(p2 + dtb)
        gc_all = jnp.dot(tril_f, g_all, preferred_element_type=F32, precision=lax.Precision.HIGHEST)
        gc_ref[c] = gc_all
        gct_ref[c] = gc_all.T
        beta_ref[c] = _sigmoid(p2)

    for i, (c, h) in enumerate(items):
        rows = slice(c * CHUNK, (c + 1) * CHUNK)
        cols = slice(h * B_HEAD_DIM, (h + 1) * B_HEAD_DIM)
        q = qkv_ref[0, rows, cols]
        k = qkv_ref[1, rows, cols]
        v = qkv_ref[2, rows, cols]
        q = q * (lax.rsqrt(jnp.sum(q * q, axis=-1, keepdims=True) + EPS) * (B_HEAD_DIM ** -0.5))
        k = k * lax.rsqrt(jnp.sum(k * k, axis=-1, keepdims=True) + EPS)
        beta = beta_ref[c, :, h:h + 1]
        gc_col = jnp.broadcast_to(gc_ref[c, :, B_HEADS + h:B_HEADS + h + 1], (CHUNK, B_HEAD_DIM))
        gc_row = gct_ref[c, B_HEADS + h:B_HEADS + h + 1, :]
        decay = jnp.exp(jnp.where(incl, gc_col - gc_row, -jnp.inf))
        kb = k * beta
        vb = v * beta
        egc = jnp.exp(gc_col)
        aq = _dot_nt(jnp.concatenate([kb, q], axis=0).astype(BF16), k.astype(BF16))
        m = -jnp.where(strict, aq[lo] * decay, 0.0)
        r_ref[i] = m
        stack_ref[i, lo] = m.astype(BF16)
        attn_ref[i] = (aq[hi] * decay).astype(BF16)
        rhs = jnp.concatenate([vb, kb * egc], axis=1)
        rhs_ref[i] = rhs
        rhsb_ref[i] = rhs.astype(BF16)
        wq_ref[i, hi] = (q * egc).astype(BF16)
        g_last = gc_col[CHUNK - 1:CHUNK, :]
        kt_ref[i] = (k * jnp.exp(g_last - gc_col)).T.astype(BF16)
        eg_ref[i] = jnp.exp(g_last)

    for i in range(len(items)):
        mb = stack_ref[i, lo]
        p = _dot(mb, mb)
        p_ref[i] = p
        stack_ref[i, hi] = p.astype(BF16)
    for _ in range(5):
        for i in range(len(items)):
            rp = _dot(stack_ref[i], stack_ref[i, hi])
            r = r_ref[i] + p_ref[i] + rp[lo]
            p = rp[hi]
            r_ref[i] = r
            p_ref[i] = p
            stack_ref[i, lo] = r.astype(BF16)
            stack_ref[i, hi] = p.astype(BF16)
    for i in range(len(items)):
        r = r_ref[i] + p_ref[i] + _dot(stack_ref[i, lo], stack_ref[i, hi])
        sol = rhs_ref[i] + _dot(r.astype(BF16), rhsb_ref[i])
        u_ref[i] = sol[:, :B_HEAD_DIM]
        wq_ref[i, lo] = sol[:, B_HEAD_DIM:].astype(BF16)

    for c in range(nchunk):
        rows = slice(c * CHUNK, (c + 1) * CHUNK)
        for h in range(B_HEADS):
            i = c * B_HEADS + h
            wq = _dot(wq_ref[i], s_ref[h].astype(BF16))
            vn_ref[h] = (u_ref[i] - wq[lo]).astype(BF16)
            op_ref[h] = wq[hi]
        for h in range(B_HEADS):
            i = c * B_HEADS + h
            cols = slice(h * B_HEAD_DIM, (h + 1) * B_HEAD_DIM)
            vnb = vn_ref[h]
            o = op_ref[h] + _dot(attn_ref[i], vnb)
            s_ref[h] = s_ref[h] * eg_ref[i] + _dot(kt_ref[i], vnb)
            z = z_ref[rows, cols].astype(F32)
            on = o * lax.rsqrt(jnp.mean(o * o, axis=-1, keepdims=True) + EPS) * ng
            o_ref[rows, cols] = (on * _silu(z)).astype(o_ref.dtype)


def _deltanet(proj, p2, conv_w, a_log, dt_bias, norm_g, batch, seq):
    tb = min(TB_DELTA, seq)
    nt = seq // tb
    nchunk = tb // CHUNK
    ni = nchunk * B_HEADS
    dk = B_HEAD_DIM

    def col(off):
        return pl.BlockSpec((tb, WBLK), lambda b, t: (b * nt + t, off // WBLK))

    pad = LANES - 2 * B_HEADS
    alog = jnp.pad(a_log.astype(F32), (B_HEADS, pad)).reshape(1, LANES)
    dtb = jnp.pad(dt_bias.astype(F32), (B_HEADS, pad)).reshape(1, LANES)
    return pl.pallas_call(
        _deltanet_kernel,
        grid=(batch, nt),
        in_specs=[col(BQ_OFF), col(BK_OFF), col(BV_OFF), col(BZ_OFF),
                  pl.BlockSpec((tb, LANES), lambda b, t: (b * nt + t, 0)),
                  pl.BlockSpec((B_CONV, 3 * B_WIDTH), lambda b, t: (0, 0)),
                  pl.BlockSpec((1, LANES), lambda b, t: (0, 0)),
                  pl.BlockSpec((1, LANES), lambda b, t: (0, 0)),
                  pl.BlockSpec((1, B_HEAD_DIM), lambda b, t: (0, 0))],
        out_specs=pl.BlockSpec((tb, B_WIDTH), lambda b, t: (b * nt + t, 0)),
        out_shape=jax.ShapeDtypeStruct((batch * seq, B_WIDTH), BF16),
        scratch_shapes=[pltpu.VMEM((B_HEADS, dk, dk), F32),
                        pltpu.VMEM((3, SUBLANES, B_WIDTH), F32),
                        pltpu.VMEM((3, tb, B_WIDTH), F32),
                        pltpu.VMEM((nchunk, CHUNK, LANES), F32),
                        pltpu.VMEM((nchunk, LANES, CHUNK), F32),
                        pltpu.VMEM((nchunk, CHUNK, LANES), F32),
                        pltpu.VMEM((ni, CHUNK, CHUNK), F32),
                        pltpu.VMEM((ni, CHUNK, CHUNK), F32),
                        pltpu.VMEM((ni, 2 * CHUNK, CHUNK), BF16),
                        pltpu.VMEM((ni, CHUNK, CHUNK), BF16),
                        pltpu.VMEM((ni, CHUNK, 2 * dk), F32),
                        pltpu.VMEM((ni, CHUNK, 2 * dk), BF16),
                        pltpu.VMEM((ni, CHUNK, dk), F32),
                        pltpu.VMEM((ni, 2 * CHUNK, dk), BF16),
                        pltpu.VMEM((ni, dk, CHUNK), BF16),
                        pltpu.VMEM((ni, 1, dk), F32),
                        pltpu.VMEM((B_HEADS, CHUNK, dk), BF16),
                        pltpu.VMEM((B_HEADS, CHUNK, dk), F32)],
        compiler_params=_cparams("parallel", "arbitrary"),
        name="deltanet",
    )(proj, proj, proj, proj, p2, conv_w.astype(F32), alog, dtb, norm_g.reshape(1, -1).astype(F32))


def _t5_bucket(dist):
    max_exact = REL_BUCKETS // 2
    n = np.maximum(dist, 0)
    large = max_exact + (np.log(np.maximum(n, 1) / max_exact) / math.log(REL_MAX_DIST / max_exact)
                         * (REL_BUCKETS - max_exact)).astype(np.int32)
    large = np.minimum(large, REL_BUCKETS - 1)
    return np.where(n < max_exact, n, large).astype(np.int32)


def _band_pattern(window, dil):
    i = np.arange(C_BLOCK)[:, None]
    kk = np.arange(2 * C_BLOCK)[None, :]
    j = C_BLOCK + i - kk
    band = (j >= 0) & (j <= window // dil)
    bucket = _t5_bucket(np.clip(j, 0, None) * dil)
    return band, bucket


def _attn_kernel(q0_ref, q1_ref, q2_ref, k0_ref, k1_ref, k2_ref, kp0_ref, kp1_ref, kp2_ref, v_ref, vp_ref,
                 bias_ref, o_ref, s_ref, p_ref, inv_ref, og_ref, lse_ref):
    tt = q0_ref.shape[0]
    first = pl.program_id(1) == 0
    q_refs = (q0_ref, q1_ref, q2_ref)
    k_refs = (k0_ref, k1_ref, k2_ref)
    kp_refs = (kp0_ref, kp1_ref, kp2_ref)
    scale = C_HEAD_DIM ** -0.5
    shape = (C_BLOCK, C_HEAD_DIM)
    lo, hi = slice(0, C_BLOCK), slice(C_BLOCK, 2 * C_BLOCK)

    for g, (_, dil) in enumerate(C_PATTERNS):
        q_ref, k_ref, kp_ref = q_refs[g], k_refs[g], kp_refs[g]
        span = C_BLOCK * dil
        nblk = tt // span
        bias_p = bias_ref[g, 0, :, lo]
        bias_c = bias_ref[g, 0, :, hi]

        def rows(start, dil=dil):
            return pl.ds(start, C_BLOCK, stride=dil) if dil > 1 else pl.ds(start, C_BLOCK)

        blocks = [(rows(r + n * span), rows(r + (n - 1) * span) if n else rows(r + (nblk - 1) * span), n == 0)
                  for r in range(dil) for n in range(nblk)]

        for j, (sl, psl, from_prev_tile) in enumerate(blocks):
            q = (q_ref[sl, :] * scale).astype(BF16)
            kp = (kp_ref if from_prev_tile else k_ref)[psl, :].astype(BF16)
            s_p = _dot_nt(q, kp) + bias_p
            if from_prev_tile:
                s_p = jnp.where(first, -jnp.inf, s_p)
            s_ref[j, :, lo] = s_p
            s_ref[j, :, hi] = _dot_nt(q, k_ref[sl, :].astype(BF16)) + bias_c

        for j, (sl, psl, from_prev_tile) in enumerate(blocks):
            s = s_ref[j]
            mx = jnp.max(s, axis=-1, keepdims=True)
            p = jnp.exp(s - mx)
            den = jnp.sum(p, axis=-1, keepdims=True)
            p_ref[j] = p.astype(BF16)
            inv_ref[j] = jnp.broadcast_to(1.0 / den, shape)
            lse_ref[g, sl, :] = jnp.broadcast_to(mx + jnp.log(den), shape)

        for j, (sl, psl, from_prev_tile) in enumerate(blocks):
            vp = (vp_ref if from_prev_tile else v_ref)[psl, :].astype(BF16)
            num = _dot(p_ref[j, :, lo], vp) + _dot(p_ref[j, :, hi], v_ref[sl, :].astype(BF16))
            og_ref[g, sl, :] = num * inv_ref[j]

    piece = 2 * C_BLOCK
    for i in range(tt // piece):
        sl = slice(i * piece, (i + 1) * piece)
        l0, l1, l2 = lse_ref[0, sl, :], lse_ref[1, sl, :], lse_ref[2, sl, :]
        lm = jnp.maximum(jnp.maximum(l0, l1), l2)
        e0, e1, e2 = jnp.exp(l0 - lm), jnp.exp(l1 - lm), jnp.exp(l2 - lm)
        num = e0 * og_ref[0, sl, :] + e1 * og_ref[1, sl, :] + e2 * og_ref[2, sl, :]
        o_ref[sl, :] = (num / (e0 + e1 + e2)).astype(o_ref.dtype)


def _dilated_attn(cproj, rel_bias, batch, seq):
    tt = TT_ATTN
    assert seq % tt == 0 and tt == C_BLOCK * max(d for _, d in C_PATTERNS)
    nt = seq // tt
    biases = []
    for gi, (window, dil) in enumerate(C_PATTERNS):
        band, bucket = _band_pattern(window, dil)
        onehot = jnp.asarray(bucket[..., None] == np.arange(REL_BUCKETS), F32)
        table = rel_bias[:, gi * C_HEADS:(gi + 1) * C_HEADS].astype(F32)
        bias = jnp.einsum('qkn,nh->hqk', onehot, table, precision=lax.Precision.HIGHEST)
        biases.append(jnp.where(band[None], bias, -jnp.inf))
    bias = jnp.stack(biases)

    def cur(off, g=0):
        c0 = off // C_HEAD_DIM + g * C_HEADS
        return pl.BlockSpec((tt, C_HEAD_DIM), lambda b, t, h: (b * nt + t, c0 + h))

    def prev(off, g=0):
        c0 = off // C_HEAD_DIM + g * C_HEADS
        return pl.BlockSpec((tt, C_HEAD_DIM), lambda b, t, h: (b * nt + jnp.maximum(t - 1, 0), c0 + h))

    ng = len(C_PATTERNS)
    nblocks = tt // C_BLOCK
    in_specs = ([cur(CQ_OFF, g) for g in range(ng)] + [cur(CK_OFF, g) for g in range(ng)]
                + [prev(CK_OFF, g) for g in range(ng)] + [cur(CV_OFF), prev(CV_OFF)]
                + [pl.BlockSpec((ng, 1, C_BLOCK, 2 * C_BLOCK), lambda b, t, h: (0, h, 0, 0))])
    return pl.pallas_call(
        _attn_kernel,
        grid=(batch, nt, C_HEADS),
        in_specs=in_specs,
        out_specs=pl.BlockSpec((tt, C_HEAD_DIM), lambda b, t, h: (b * nt + t, h)),
        out_shape=jax.ShapeDtypeStruct((batch * seq, C_WIDTH), BF16),
        scratch_shapes=[pltpu.VMEM((nblocks, C_BLOCK, 2 * C_BLOCK), F32),
                        pltpu.VMEM((nblocks, C_BLOCK, 2 * C_BLOCK), BF16),
                        pltpu.VMEM((nblocks, C_BLOCK, C_HEAD_DIM), F32),
                        pltpu.VMEM((ng, tt, C_HEAD_DIM), F32),
                        pltpu.VMEM((ng, tt, C_HEAD_DIM), F32)],
        compiler_params=_cparams("parallel", "arbitrary", "arbitrary"),
        name="dilated_attn",
    )(*([cproj] * (3 * ng + 2)), bias)


def _merge_kernel(x_ref, g0_ref, g1_ref, g2_ref, ya_ref, yb_ref, yc_ref, wb_ref, wo_ref, out_ref):
    merged = (_sigmoid(g0_ref[...].astype(F32)) * _dot(ya_ref[...], wb_ref[0])
              + _sigmoid(g1_ref[...].astype(F32)) * _dot(yb_ref[...], wb_ref[1])
              + _sigmoid(g2_ref[...].astype(F32)) * _dot(yc_ref[...], wb_ref[2]))
    out_ref[...] = x_ref[...] + _dot(merged.astype(BF16), wo_ref[...])


def _merge(x2d, proj, ya, yb, yc, w_branch, w_out):
    n, d = x2d.shape
    tm = min(TM_MERGE, n)
    row = lambda w: pl.BlockSpec((tm, w), lambda i: (i, 0))
    gate = lambda g: pl.BlockSpec((tm, d), lambda i: (i, GATE_OFF // d + g))
    return pl.pallas_call(
        _merge_kernel,
        grid=(n // tm,),
        in_specs=[row(d), gate(0), gate(1), gate(2), row(WBLK), row(WBLK), row(WBLK),
                  pl.BlockSpec((3, WBLK, d), lambda i: (0, 0, 0)),
                  pl.BlockSpec((d, d), lambda i: (0, 0))],
        out_specs=row(d),
        out_shape=jax.ShapeDtypeStruct((n, d), F32),
        compiler_params=_cparams("parallel"),
        name="merge",
    )(x2d, proj, proj, proj, ya, yb, yc, w_branch.astype(BF16), w_out.astype(BF16))


EXP_LANE0 = MOE_GROUPS
META_E, META_RANK, META_GATE = 0, 2, 4


def _split3_dot(h, w_hi, w_lo):
    h_hi = h.astype(BF16)
    h_lo = (h - h_hi.astype(F32)).astype(BF16)
    return _dot(h_hi, w_hi) + (_dot(h_lo, w_hi) + _dot(h_hi, w_lo))


def _router_kernel(x_ref, g_ref, whi_ref, wlo_ref, meta_ref, cnt_ref, run_ref):
    tm = x_ref.shape[0]

    @pl.when(pl.program_id(0) == 0)
    def _():
        run_ref[...] = jnp.zeros_like(run_ref)

    x = x_ref[...]
    h = x * lax.rsqrt(jnp.mean(x * x, axis=-1, keepdims=True) + EPS) * g_ref[...]
    logits = _split3_dot(h, whi_ref[...], wlo_ref[...])
    lane = lax.broadcasted_iota(jnp.int32, (tm, LANES), 1)
    lane_f = lane.astype(F32)
    big = float(LANES)

    def first_argmax(vals, mx):
        return jnp.min(jnp.where(vals == mx, lane_f, big), axis=-1, keepdims=True)

    gl = jnp.where(lane < MOE_GROUPS, logits, -jnp.inf)
    gmax = jnp.max(gl, axis=-1, keepdims=True)
    grp_p = 1.0 / jnp.sum(jnp.exp(gl - gmax), axis=-1, keepdims=True)
    gidx = first_argmax(gl, gmax)
    lo = EXP_LANE0 + gidx * MOE_PER_GROUP
    in_grp = (lane_f >= lo) & (lane_f < lo + MOE_PER_GROUP)
    el = jnp.where(in_grp, logits, -jnp.inf)
    m1 = jnp.max(el, axis=-1, keepdims=True)
    i1 = first_argmax(el, m1)
    el2 = jnp.where(lane_f == i1, -jnp.inf, el)
    m2 = jnp.max(el2, axis=-1, keepdims=True)
    i2 = first_argmax(el2, m2)
    t2 = jnp.exp(m2 - m1)
    gate1 = grp_p / (1.0 + t2)
    gate2 = grp_p * t2 / (1.0 + t2)
    oh1 = lane_f == i1
    oh2 = lane_f == i2
    onehot = jnp.where(oh1 | oh2, 1.0, 0.0)
    ri = lax.broadcasted_iota(jnp.int32, (tm, tm), 0)
    ci = lax.broadcasted_iota(jnp.int32, (tm, tm), 1)
    before = jnp.where(ri > ci, 1.0, 0.0).astype(BF16)
    prior = _dot(before, onehot.astype(BF16)) + run_ref[0:1, :]
    rank1 = jnp.sum(jnp.where(oh1, prior, 0.0), axis=-1, keepdims=True)
    rank2 = jnp.sum(jnp.where(oh2, prior, 0.0), axis=-1, keepdims=True)
    run_new = run_ref[0:1, :] + jnp.sum(onehot, axis=0, keepdims=True)
    run_ref[0:1, :] = run_new
    cnt_ref[...] = jnp.broadcast_to(run_new, cnt_ref.shape)
    rec = jnp.zeros((tm, LANES), F32)
    for ln, val in ((META_E, i1 - EXP_LANE0), (META_E + 1, i2 - EXP_LANE0), (META_RANK, rank1),
                    (META_RANK + 1, rank2), (META_GATE, gate1), (META_GATE + 1, gate2)):
        rec = jnp.where(lane == ln, val, rec)
    meta_ref[...] = rec


def _router(x2d, g, w_group, w_expert):
    n, d = x2d.shape
    tm = min(TM_ROUTER, n)
    w = jnp.concatenate([w_group, w_expert], axis=1).astype(F32)
    w = jnp.pad(w, ((0, 0), (0, LANES - w.shape[1])))
    w_hi = w.astype(BF16)
    w_lo = (w - w_hi.astype(F32)).astype(BF16)
    return pl.pallas_call(
        _router_kernel,
        grid=(n // tm,),
        in_specs=[pl.BlockSpec((tm, d), lambda i: (i, 0)),
                  pl.BlockSpec((1, d), lambda i: (0, 0)),
                  pl.BlockSpec((d, LANES), lambda i: (0, 0)),
                  pl.BlockSpec((d, LANES), lambda i: (0, 0))],
        out_specs=[pl.BlockSpec((tm, LANES), lambda i: (i, 0)),
                   pl.BlockSpec((SUBLANES, LANES), lambda i: (0, 0))],
        out_shape=[jax.ShapeDtypeStruct((n, LANES), F32),
                   jax.ShapeDtypeStruct((SUBLANES, LANES), F32)],
        scratch_shapes=[pltpu.VMEM((SUBLANES, LANES), F32)],
        compiler_params=_cparams("arbitrary"),
        name="router",
    )(x2d, g.reshape(1, d), w_hi, w_lo)


DMA_UNROLL = 8


ROW_SUB = D_MODEL // LANES
assert ROW_SUB == SUBLANES


def _row_tile_store(ref, y):
    m = y.shape[0]
    for s in range(ROW_SUB):
        ref[pl.ds(s, m, stride=ROW_SUB), :] = y[:, s * LANES:(s + 1) * LANES]


def _row_tile_load(ref, m, s):
    return ref[pl.ds(s, m, stride=ROW_SUB), :]


def _row_of(ref, i):
    return ref.at[pl.ds(pl.multiple_of(i * ROW_SUB, ROW_SUB), ROW_SUB)]


def _start_rows(row_copy, m):
    def body(j, c):
        row_copy(j, 0).start()
        row_copy(j, 1).start()
        return c
    lax.fori_loop(0, m, body, 0, unroll=DMA_UNROLL)


def _wait_rows(row_copy, m):
    def body(j, c):
        row_copy(j, 0).wait()
        row_copy(j, 1).wait()
        return c
    lax.fori_loop(0, m, body, 0, unroll=DMA_UNROLL)


def _dispatch_kernel(dest_ref, x_ref, xs_init_hbm, xs_hbm, xta_ref, xtb_ref, sem_a, sem_b):
    del xs_init_hbm
    th = x_ref.shape[0] // 2
    i = pl.program_id(0)

    def copy_a(j, k):
        return pltpu.make_async_copy(_row_of(xta_ref, j), xs_hbm.at[dest_ref[2 * j + k]], sem_a)

    def copy_b(j, k):
        return pltpu.make_async_copy(_row_of(xtb_ref, j), xs_hbm.at[dest_ref[2 * th + 2 * j + k]], sem_b)

    @pl.when(i > 0)
    def _():
        _wait_rows(copy_a, th)
    _row_tile_store(xta_ref, x_ref[:th, :])
    _start_rows(copy_a, th)

    @pl.when(i > 0)
    def _():
        _wait_rows(copy_b, th)
    _row_tile_store(xtb_ref, x_ref[th:, :])
    _start_rows(copy_b, th)

    @pl.when(i == pl.num_programs(0) - 1)
    def _():
        _wait_rows(copy_a, th)
        _wait_rows(copy_b, th)


def _dispatch(x2d, dest, rows):
    n, d = x2d.shape
    td = min(TD_DISPATCH, n)
    return pl.pallas_call(
        _dispatch_kernel,
        grid=(n // td,),
        in_specs=[pl.BlockSpec((2 * td,), lambda i: (i,), memory_space=pltpu.SMEM),
                  pl.BlockSpec((td, d), lambda i: (i, 0)),
                  pl.BlockSpec(memory_space=pl.ANY)],
        out_specs=pl.BlockSpec(memory_space=pl.ANY),
        out_shape=jax.ShapeDtypeStruct((rows, ROW_SUB, LANES), F32),
        scratch_shapes=[pltpu.VMEM((td // 2 * ROW_SUB, LANES), F32), pltpu.VMEM((td // 2 * ROW_SUB, LANES), F32),
                        pltpu.SemaphoreType.DMA, pltpu.SemaphoreType.DMA],
        input_output_aliases={2: 0},
        compiler_params=_cparams("arbitrary"),
        name="moe_dispatch",
    )(dest, x2d, jnp.zeros((rows, ROW_SUB, LANES), F32))


def _experts_kernel(be_ref, nused_ref, xs_ref, g_ref, w1_ref, w3_ref, w2_ref, ys_ref, w1b_ref, w3b_ref, w2b_ref):
    i = pl.program_id(0)
    bm = xs_ref.shape[0] // ROW_SUB

    @pl.when(jnp.logical_or(i == 0, be_ref[i] != be_ref[jnp.maximum(i - 1, 0)]))
    def _():
        w1b_ref[...] = w1_ref[0].astype(BF16)
        w3b_ref[...] = w3_ref[0].astype(BF16)
        w2b_ref[...] = w2_ref[0].astype(BF16)

    @pl.when(i < nused_ref[0])
    def _():
        x = jnp.concatenate([_row_tile_load(xs_ref, bm, s) for s in range(ROW_SUB)], axis=1)
        h = (x * lax.rsqrt(jnp.mean(x * x, axis=-1, keepdims=True) + EPS) * g_ref[...]).astype(BF16)
        hid = _silu(_dot(h, w1b_ref[...])) * _dot(h, w3b_ref[...])
        _row_tile_store(ys_ref, _dot(hid.astype(BF16), w2b_ref[...]))

    @pl.when(i >= nused_ref[0])
    def _():
        ys_ref[...] = jnp.zeros_like(ys_ref)


def _experts(xs, g, blk_expert, n_used, w1, w3, w2):
    rows = xs.shape[0] // ROW_SUB
    d = D_MODEL
    nb = rows // BM_EXPERT
    ff = w1.shape[-1]
    blk = BM_EXPERT * ROW_SUB
    grid_spec = pltpu.PrefetchScalarGridSpec(
        num_scalar_prefetch=2,
        grid=(nb,),
        in_specs=[pl.BlockSpec((blk, LANES), lambda i, be, nu: (i, 0)),
                  pl.BlockSpec((1, d), lambda i, be, nu: (0, 0)),
                  pl.BlockSpec((1, d, ff), lambda i, be, nu: (be[i], 0, 0)),
                  pl.BlockSpec((1, d, ff), lambda i, be, nu: (be[i], 0, 0)),
                  pl.BlockSpec((1, ff, d), lambda i, be, nu: (be[i], 0, 0))],
        out_specs=pl.BlockSpec((blk, LANES), lambda i, be, nu: (i, 0)),
        scratch_shapes=[pltpu.VMEM((d, ff), BF16), pltpu.VMEM((d, ff), BF16), pltpu.VMEM((ff, d), BF16)],
    )
    return pl.pallas_call(
        _experts_kernel,
        grid_spec=grid_spec,
        out_shape=jax.ShapeDtypeStruct((rows * ROW_SUB, LANES), F32),
        compiler_params=_cparams("arbitrary"),
        name="moe_experts",
    )(blk_expert, n_used, xs, g.reshape(1, d), w1, w3, w2)


def _combine_kernel(final, dest_ref, dest_next_ref, x_ref, meta_ref, ys_hbm, g_ref, o_ref,
                    ya0_ref, ya1_ref, yb0_ref, yb1_ref, sem_a, sem_b):
    th = x_ref.shape[0] // 2
    i = pl.program_id(0)

    def gather(dref, off, bufs, sem):
        def row_copy(j, k):
            return pltpu.make_async_copy(ys_hbm.at[dref[off + 2 * j + k]], _row_of(bufs[k], j), sem)
        return row_copy

    copy_a = gather(dest_ref, 0, (ya0_ref, ya1_ref), sem_a)
    copy_b = gather(dest_ref, 2 * th, (yb0_ref, yb1_ref), sem_b)
    copy_a_next = gather(dest_next_ref, 0, (ya0_ref, ya1_ref), sem_a)

    def combine(half, y0_ref, y1_ref):
        rows = slice(half * th, (half + 1) * th)
        meta = meta_ref[rows, :]
        gate0 = meta[:, META_GATE:META_GATE + 1]
        gate1 = meta[:, META_GATE + 1:META_GATE + 2]
        cols = [slice(s * LANES, (s + 1) * LANES) for s in range(ROW_SUB)]
        ys = [x_ref[rows, cols[s]] + gate0 * _row_tile_load(y0_ref, th, s) + gate1 * _row_tile_load(y1_ref, th, s)
              for s in range(ROW_SUB)]
        if final:
            ssq = sum(jnp.sum(y * y, axis=-1, keepdims=True) for y in ys)
            inv = lax.rsqrt(ssq * (1.0 / D_MODEL) + EPS)
            ys = [y * inv * g_ref[:, cols[s]] for s, y in enumerate(ys)]
        for s, y in enumerate(ys):
            o_ref[rows, cols[s]] = y

    @pl.when(i == 0)
    def _():
        _start_rows(copy_a, th)
    _start_rows(copy_b, th)
    _wait_rows(copy_a, th)
    combine(0, ya0_ref, ya1_ref)

    @pl.when(i + 1 < pl.num_programs(0))
    def _():
        _start_rows(copy_a_next, th)
    _wait_rows(copy_b, th)
    combine(1, yb0_ref, yb1_ref)


def _combine(x2d, meta, dest, ys, final_g, final):
    n, d = x2d.shape
    tc = min(TC_COMBINE, n)
    nsteps = n // tc
    half_buf = pltpu.VMEM((tc // 2 * ROW_SUB, LANES), F32)
    return pl.pallas_call(
        functools.partial(_combine_kernel, final),
        grid=(nsteps,),
        in_specs=[pl.BlockSpec((2 * tc,), lambda i: (i,), memory_space=pltpu.SMEM),
                  pl.BlockSpec((2 * tc,), lambda i: (jnp.minimum(i + 1, nsteps - 1),), memory_space=pltpu.SMEM),
                  pl.BlockSpec((tc, d), lambda i: (i, 0)),
                  pl.BlockSpec((tc, LANES), lambda i: (i, 0)),
                  pl.BlockSpec(memory_space=pl.ANY),
                  pl.BlockSpec((1, d), lambda i: (0, 0))],
        out_specs=pl.BlockSpec((tc, d), lambda i: (i, 0)),
        out_shape=jax.ShapeDtypeStruct((n, d), F32),
        scratch_shapes=[half_buf, half_buf, half_buf, half_buf,
                        pltpu.SemaphoreType.DMA, pltpu.SemaphoreType.DMA],
        compiler_params=_cparams("arbitrary"),
        name="moe_combine_final" if final else "moe_combine",
    )(dest, dest, x2d, meta, ys, final_g.reshape(1, d))


def _moe(x2d, norm_g, w_group, w_expert, w1, w3, w2, final_g, final):
    n, d = x2d.shape
    meta, cnt = _router(x2d, norm_g, w_group, w_expert)
    counts = cnt[0, EXP_LANE0:EXP_LANE0 + MOE_EXPERTS].astype(jnp.int32)
    padded = (counts + BM_EXPERT - 1) // BM_EXPERT * BM_EXPERT
    pad_end = jnp.cumsum(padded)
    pad_start = pad_end - padded
    e = meta[:, META_E:META_E + 2].astype(jnp.int32)
    rank = meta[:, META_RANK:META_RANK + 2].astype(jnp.int32)
    eids = jnp.arange(MOE_EXPERTS, dtype=jnp.int32)
    dest = (jnp.sum(jnp.where(e[..., None] == eids, pad_start, 0), axis=-1) + rank).reshape(2 * n)
    rows = 2 * n + MOE_EXPERTS * BM_EXPERT
    nb = rows // BM_EXPERT
    blk_start = jnp.arange(nb, dtype=jnp.int32) * BM_EXPERT
    blk_expert = jnp.minimum(jnp.sum((pad_end[None, :] <= blk_start[:, None]).astype(jnp.int32), axis=1),
                             MOE_EXPERTS - 1)
    n_used = (pad_end[-1:] // BM_EXPERT).astype(jnp.int32)
    xs = _dispatch(x2d, dest, rows)
    ys = _experts(xs.reshape(rows * ROW_SUB, LANES), norm_g, blk_expert, n_used, w1, w3, w2)
    return _combine(x2d, meta, dest, ys.reshape(rows, ROW_SUB, LANES), final_g, final)


def _split_w_in(w):
    n_ab = 2 * A_WIDTH + 4 * B_WIDTH
    n_c = n_ab + 2 * B_HEADS
    n_g = n_c + 2 * 3 * C_WIDTH + C_WIDTH
    w_main = jnp.concatenate([w[:, n_g:], w[:, :n_ab]], axis=1).astype(BF16)
    w_c = w[:, n_c:n_g].astype(BF16)
    w_small = jnp.pad(w[:, n_ab:n_c], ((0, 0), (0, LANES - 2 * B_HEADS))).astype(BF16)
    return w_main, w_c, w_small


def kernel(x, norm_mix_g, w_in, a_ln_g, a_ln_b, a_w_s, a_b_s, b_conv_w, b_a_log, b_dt_bias, b_norm_g, rel_bias,
           w_branch, w_out, norm_ffn_g, w_group, w_expert, w1, w3, w2, final_norm_g):
    batch, seq, d = x.shape
    depth = w_in.shape[0]
    x2d = x.reshape(batch * seq, d)
    for l in range(depth):
        w_main, w_c, w_small = _split_w_in(w_in[l])
        assert w_main.shape[1] == PROJ_W and w_c.shape[1] == CPROJ_W
        proj = _rms_matmul(x2d, norm_mix_g[l], w_main, BF16, TN_PROJ_MAIN)
        cproj = _rms_matmul(x2d, norm_mix_g[l], w_c, F32, TN_PROJ)
        p2 = _rms_matmul(x2d, norm_mix_g[l], w_small, F32, LANES)
        ya = _gmlp(proj, a_w_s[l], a_b_s[l], a_ln_g[l], a_ln_b[l])
        yb = _deltanet(proj, p2, b_conv_w[l], b_a_log[l], b_dt_bias[l], b_norm_g[l], batch, seq)
        yc = _dilated_attn(cproj, rel_bias, batch, seq)
        x2d = _merge(x2d, proj, ya, yb, yc, w_branch[l], w_out[l])
        x2d = _moe(x2d, norm_ffn_g[l], w_group[l], w_expert[l], w1[l], w3[l], w2[l], final_norm_g,
                   final=(l == depth - 1))
    return x2d.reshape(batch, seq, d)
```

```python
import functools
import math

import numpy as np
import jax
import jax.numpy as jnp
from jax import lax
from jax.experimental import pallas as pl
from jax.experimental.pallas import tpu as pltpu

F32 = jnp.float32
BF16 = jnp.bfloat16

D_MODEL = 1024
A_GROUPS, A_CHUNK, A_WIDTH = 4, 128, 512
B_HEADS, B_HEAD_DIM, B_CONV, B_WIDTH = 4, 128, 4, 512
C_PATTERNS = ((128, 1), (512, 4), (2048, 16))
C_HEADS, C_HEAD_DIM, C_BLOCK, C_WIDTH = 4, 128, 128, 512
REL_BUCKETS, REL_MAX_DIST = 32, 2048
MOE_GROUPS, MOE_PER_GROUP, MOE_EXPERTS, MOE_FF = 4, 8, 32, 512
EPS = 1e-6

LANES = 128
SUBLANES = 8
VMEM_LIMIT = 48 * 1024 * 1024

PROJ_W = 6144
GATE_OFF, AU_OFF, AV_OFF = 0, 3072, 3584
BQ_OFF, BK_OFF, BV_OFF, BZ_OFF = 4096, 4608, 5120, 5632
WBLK = 512
CPROJ_W = 3584
CQ_OFF, CK_OFF, CV_OFF = 0, 1536, 3072

TM_PROJ, TN_PROJ, TN_PROJ_MAIN = 2048, 512, 1024
TM_GMLP = 512
TB_DELTA = 512
CHUNK = 128
TT_ATTN = 2048
TM_MERGE = 512
TM_ROUTER = 512
TD_DISPATCH = 512
BM_EXPERT = 512
TC_COMBINE = 512


def _cparams(*sem):
    return pltpu.CompilerParams(dimension_semantics=sem, vmem_limit_bytes=VMEM_LIMIT)


def _dot(a, b):
    return jnp.dot(a, b, preferred_element_type=F32)


def _dot_nt(a, b):
    return lax.dot_general(a, b, (((1,), (1,)), ((), ())), preferred_element_type=F32)


def _sigmoid(x):
    return 1.0 / (1.0 + jnp.exp(-x))


def _silu(x):
    return x * _sigmoid(x)


def _gelu_tanh(x):
    c = math.sqrt(2.0 / math.pi)
    return x * (0.5 * (1.0 + jnp.tanh(c * (x + 0.044715 * (x * x * x)))))


def _softplus(x):
    return jnp.maximum(x, 0.0) + jnp.log(1.0 + jnp.exp(-jnp.abs(x)))


def _rms_matmul_kernel(x_ref, g_ref, w_ref, o_ref, h_ref):
    @pl.when(pl.program_id(1) == 0)
    def _():
        x = x_ref[...]
        ms = jnp.mean(x * x, axis=-1, keepdims=True)
        h_ref[...] = (x * lax.rsqrt(ms + EPS) * g_ref[...]).astype(h_ref.dtype)

    o_ref[...] = _dot(h_ref[...], w_ref[...]).astype(o_ref.dtype)


def _rms_matmul(x2d, g, w, out_dtype, tn):
    n, d = x2d.shape
    ncols = w.shape[1]
    tm = min(TM_PROJ, n)
    return pl.pallas_call(
        _rms_matmul_kernel,
        grid=(n // tm, ncols // tn),
        in_specs=[pl.BlockSpec((tm, d), lambda i, j: (i, 0)),
                  pl.BlockSpec((1, d), lambda i, j: (0, 0)),
                  pl.BlockSpec((d, tn), lambda i, j: (0, j))],
        out_specs=pl.BlockSpec((tm, tn), lambda i, j: (i, j)),
        out_shape=jax.ShapeDtypeStruct((n, ncols), out_dtype),
        scratch_shapes=[pltpu.VMEM((tm, d), BF16)],
        compiler_params=_cparams("parallel", "arbitrary"),
        name="rms_matmul",
    )(x2d, g.reshape(1, d), w)


def _gmlp_kernel(u_ref, v_ref, w_ref, b_ref, lng_ref, lnb_ref, o_ref):
    tm = u_ref.shape[0]
    u = _gelu_tanh(u_ref[...].astype(F32))
    v = _gelu_tanh(v_ref[...].astype(F32))
    mu = jnp.mean(v, axis=-1, keepdims=True)
    vc = v - mu
    var = jnp.mean(vc * vc, axis=-1, keepdims=True)
    vn = (vc * lax.rsqrt(var + EPS) * lng_ref[...] + lnb_ref[...]).astype(BF16)
    for c in range(tm // A_CHUNK):
        rows = slice(c * A_CHUNK, (c + 1) * A_CHUNK)
        for g in range(A_GROUPS):
            cols = slice(g * LANES, (g + 1) * LANES)
            sv = _dot(w_ref[g], vn[rows, cols]) + b_ref[:, cols]
            o_ref[rows, cols] = (u[rows, cols] * sv).astype(o_ref.dtype)


def _gmlp(proj, w_s, b_s, ln_g, ln_b):
    n = proj.shape[0]
    tm = min(TM_GMLP, n)
    causal = np.tril(np.ones((A_CHUNK, A_CHUNK), dtype=bool))
    w = jnp.where(causal, w_s, 0.0).astype(BF16)
    bias = jnp.repeat(jnp.transpose(b_s), LANES, axis=1).astype(F32)
    return pl.pallas_call(
        _gmlp_kernel,
        grid=(n // tm,),
        in_specs=[pl.BlockSpec((tm, WBLK), lambda i: (i, AU_OFF // WBLK)),
                  pl.BlockSpec((tm, WBLK), lambda i: (i, AV_OFF // WBLK)),
                  pl.BlockSpec((A_GROUPS, A_CHUNK, A_CHUNK), lambda i: (0, 0, 0)),
                  pl.BlockSpec((A_CHUNK, A_WIDTH), lambda i: (0, 0)),
                  pl.BlockSpec((1, A_WIDTH), lambda i: (0, 0)),
                  pl.BlockSpec((1, A_WIDTH), lambda i: (0, 0))],
        out_specs=pl.BlockSpec((tm, A_WIDTH), lambda i: (i, 0)),
        out_shape=jax.ShapeDtypeStruct((n, A_WIDTH), BF16),
        compiler_params=_cparams("parallel"),
        name="gmlp",
    )(proj, proj, w, bias, ln_g.reshape(1, -1), ln_b.reshape(1, -1))


def _deltanet_kernel(q_ref, k_ref, v_ref, z_ref, p2_ref, cw_ref, alog_ref, dtb_ref, ng_ref, o_ref,
                     s_ref, carry_ref, qkv_ref, gc_ref, gct_ref, beta_ref, r_ref, p_ref, stack_ref, attn_ref,
                     rhs_ref, rhsb_ref, u_ref, wq_ref, kt_ref, eg_ref, vn_ref, op_ref):
    tb = q_ref.shape[0]
    nchunk = tb // CHUNK

    @pl.when(pl.program_id(1) == 0)
    def _():
        s_ref[...] = jnp.zeros_like(s_ref)
        carry_ref[...] = jnp.zeros_like(carry_ref)

    row8 = lax.broadcasted_iota(jnp.int32, (SUBLANES, B_WIDTH), 0)
    for idx, ref in enumerate((q_ref, k_ref, v_ref)):
        x = ref[...].astype(F32)
        w = cw_ref[:, idx * B_WIDTH:(idx + 1) * B_WIDTH]
        prev = carry_ref[idx]
        acc = x * w[B_CONV - 1:B_CONV]
        for s in range(1, B_CONV):
            xs = pltpu.roll(x, s, axis=0)
            ps = pltpu.roll(prev, s, axis=0)
            head = jnp.where(row8 < s, ps, xs[:SUBLANES])
            xs = jnp.concatenate([head, xs[SUBLANES:]], axis=0)
            acc = acc + xs * w[B_CONV - 1 - s:B_CONV - s]
        carry_ref[idx] = x[tb - SUBLANES:]
        qkv_ref[idx] = _silu(acc)

    ri = lax.broadcasted_iota(jnp.int32, (CHUNK, CHUNK), 0)
    ci = lax.broadcasted_iota(jnp.int32, (CHUNK, CHUNK), 1)
    incl = ri >= ci
    strict = ri > ci
    tril_f = incl.astype(F32)
    alog = alog_ref[...]
    dtb = dtb_ref[...]
    ng = ng_ref[...]
    lo, hi = slice(0, CHUNK), slice(CHUNK, 2 * CHUNK)
    items = [(c, h) for c in range(nchunk) for h in range(B_HEADS)]

    for c in range(nchunk):
        rows = slice(c * CHUNK, (c + 1) * CHUNK)
        p2 = p2_ref[rows, :]
        g_all = -jnp.exp(alog) * _softplus(p2 + dtb)
        gc_all = jnp.dot(tril_f, g_all, preferred_element_type=F32, precision=lax.Precision.HIGHEST)
        gc_ref[c] = gc_all
        gct_ref[c] = gc_all.T
        beta_ref[c] = _sigmoid(p2)

    for i, (c, h) in enumerate(items):
        rows = slice(c * CHUNK, (c + 1) * CHUNK)
        cols = slice(h * B_HEAD_DIM, (h + 1) * B_HEAD_DIM)
        q = qkv_ref[0, rows, cols]
        k = qkv_ref[1, rows, cols]
        v = qkv_ref[2, rows, cols]
        q = q * (lax.rsqrt(jnp.sum(q * q, axis=-1, keepdims=True) + EPS) * (B_HEAD_DIM ** -0.5))
        k = k * lax.rsqrt(jnp.sum(k * k, axis=-1, keepdims=True) + EPS)
        beta = beta_ref[c, :, h:h + 1]
        gc_col = jnp.broadcast_to(gc_ref[c, :, B_HEADS + h:B_HEADS + h + 1], (CHUNK, B_HEAD_DIM))
        gc_row = gct_ref[c, B_HEADS + h:B_HEADS + h + 1, :]
        decay = jnp.exp(jnp.where(incl, gc_col - gc_row, -jnp.inf))
        kb = k * beta
        vb = v * beta
        egc = jnp.exp(gc_col)
        aq = _dot_nt(jnp.concatenate([kb, q], axis=0).astype(BF16), k.astype(BF16))
        m = -jnp.where(strict, aq[lo] * decay, 0.0)
        r_ref[i] = m
        stack_ref[i, lo] = m.astype(BF16)
        attn_ref[i] = (aq[hi] * decay).astype(BF16)
        rhs = jnp.concatenate([vb, kb * egc], axis=1)
        rhs_ref[i] = rhs
        rhsb_ref[i] = rhs.astype(BF16)
        wq_ref[i, hi] = (q * egc).astype(BF16)
        g_last = gc_col[CHUNK - 1:CHUNK, :]
        kt_ref[i] = (k * jnp.exp(g_last - gc_col)).T.astype(BF16)
        eg_ref[i] = jnp.exp(g_last)

    for i in range(len(items)):
        mb = stack_ref[i, lo]
        p = _dot(mb, mb)
        p_ref[i] = p
        stack_ref[i, hi] = p.astype(BF16)
    for _ in range(5):
        for i in range(len(items)):
            rp = _dot(stack_ref[i], stack_ref[i, hi])
            r = r_ref[i] + p_ref[i] + rp[lo]
            p = rp[hi]
            r_ref[i] = r
            p_ref[i] = p
            stack_ref[i, lo] = r.astype(BF16)
            stack_ref[i, hi] = p.astype(BF16)
    for i in range(len(items)):
        r = r_ref[i] + p_ref[i] + _dot(stack_ref[i, lo], stack_ref[i, hi])
        sol = rhs_ref[i] + _dot(r.astype(BF16), rhsb_ref[i])
        u_ref[i] = sol[:, :B_HEAD_DIM]
        wq_ref[i, lo] = sol[:, B_HEAD_DIM:].astype(BF16)

    for c in range(nchunk):
        rows = slice(c * CHUNK, (c + 1) * CHUNK)
        for h in range(B_HEADS):
            i = c * B_HEADS + h
            wq = _dot(wq_ref[i], s_ref[h].astype(BF16))
            vn_ref[h] = (u_ref[i] - wq[lo]).astype(BF16)
            op_ref[h] = wq[hi]
        for h in range(B_HEADS):
            i = c * B_HEADS + h
            cols = slice(h * B_HEAD_DIM, (h + 1) * B_HEAD_DIM)
            vnb = vn_ref[h]
            o = op_ref[h] + _dot(attn_ref[i], vnb)
            s_ref[h] = s_ref[h] * eg_ref[i] + _dot(kt_ref[i], vnb)
            z = z_ref[rows, cols].astype(F32)
            on = o * lax.rsqrt(jnp.mean(o * o, axis=-1, keepdims=True) + EPS) * ng
            o_ref[rows, cols] = (on * _silu(z)).astype(o_ref.dtype)


def _deltanet(proj, p2, conv_w, a_log, dt_bias, norm_g, batch, seq):
    tb = min(TB_DELTA, seq)
    nt = seq // tb
    nchunk = tb // CHUNK
    ni = nchunk * B_HEADS
    dk = B_HEAD_DIM

    def col(off):
        return pl.BlockSpec((tb, WBLK), lambda b, t: (b * nt + t, off // WBLK))

    pad = LANES - 2 * B_HEADS
    alog = jnp.pad(a_log.astype(F32), (B_HEADS, pad)).reshape(1, LANES)
    dtb = jnp.pad(dt_bias.astype(F32), (B_HEADS, pad)).reshape(1, LANES)
    return pl.pallas_call(
        _deltanet_kernel,
        grid=(batch, nt),
        in_specs=[col(BQ_OFF), col(BK_OFF), col(BV_OFF), col(BZ_OFF),
                  pl.BlockSpec((tb, LANES), lambda b, t: (b * nt + t, 0)),
                  pl.BlockSpec((B_CONV, 3 * B_WIDTH), lambda b, t: (0, 0)),
                  pl.BlockSpec((1, LANES), lambda b, t: (0, 0)),
                  pl.BlockSpec((1, LANES), lambda b, t: (0, 0)),
                  pl.BlockSpec((1, B_HEAD_DIM), lambda b, t: (0, 0))],
        out_specs=pl.BlockSpec((tb, B_WIDTH), lambda b, t: (b * nt + t, 0)),
        out_shape=jax.ShapeDtypeStruct((batch * seq, B_WIDTH), BF16),
        scratch_shapes=[pltpu.VMEM((B_HEADS, dk, dk), F32),
                        pltpu.VMEM((3, SUBLANES, B_WIDTH), F32),
                        pltpu.VMEM((3, tb, B_WIDTH), F32),
                        pltpu.VMEM((nchunk, CHUNK, LANES), F32),
                        pltpu.VMEM((nchunk, LANES, CHUNK), F32),
                        pltpu.VMEM((nchunk, CHUNK, LANES), F32),
                        pltpu.VMEM((ni, CHUNK, CHUNK), F32),
                        pltpu.VMEM((ni, CHUNK, CHUNK), F32),
                        pltpu.VMEM((ni, 2 * CHUNK, CHUNK), BF16),
                        pltpu.VMEM((ni, CHUNK, CHUNK), BF16),
                        pltpu.VMEM((ni, CHUNK, 2 * dk), F32),
                        pltpu.VMEM((ni, CHUNK, 2 * dk), BF16),
                        pltpu.VMEM((ni, CHUNK, dk), F32),
                        pltpu.VMEM((ni, 2 * CHUNK, dk), BF16),
                        pltpu.VMEM((ni, dk, CHUNK), BF16),
                        pltpu.VMEM((ni, 1, dk), F32),
                        pltpu.VMEM((B_HEADS, CHUNK, dk), BF16),
                        pltpu.VMEM((B_HEADS, CHUNK, dk), F32)],
        compiler_params=_cparams("parallel", "arbitrary"),
        name="deltanet",
    )(proj, proj, proj, proj, p2, conv_w.astype(F32), alog, dtb, norm_g.reshape(1, -1).astype(F32))


def _t5_bucket(dist):
    max_exact = REL_BUCKETS // 2
    n = np.maximum(dist, 0)
    large = max_exact + (np.log(np.maximum(n, 1) / max_exact) / math.log(REL_MAX_DIST / max_exact)
                         * (REL_BUCKETS - max_exact)).astype(np.int32)
    large = np.minimum(large, REL_BUCKETS - 1)
    return np.where(n < max_exact, n, large).astype(np.int32)


def _band_pattern(window, dil):
    i = np.arange(C_BLOCK)[:, None]
    kk = np.arange(2 * C_BLOCK)[None, :]
    j = C_BLOCK + i - kk
    band = (j >= 0) & (j <= window // dil)
    bucket = _t5_bucket(np.clip(j, 0, None) * dil)
    return band, bucket


def _attn_kernel(q0_ref, q1_ref, q2_ref, k0_ref, k1_ref, k2_ref, kp0_ref, kp1_ref, kp2_ref, v_ref, vp_ref,
                 bias_ref, o_ref, s_ref, p_ref, inv_ref, og_ref, lse_ref):
    tt = q0_ref.shape[0]
    first = pl.program_id(1) == 0
    q_refs = (q0_ref, q1_ref, q2_ref)
    k_refs = (k0_ref, k1_ref, k2_ref)
    kp_refs = (kp0_ref, kp1_ref, kp2_ref)
    scale = C_HEAD_DIM ** -0.5
    shape = (C_BLOCK, C_HEAD_DIM)
    lo, hi = slice(0, C_BLOCK), slice(C_BLOCK, 2 * C_BLOCK)

    for g, (_, dil) in enumerate(C_PATTERNS):
        q_ref, k_ref, kp_ref = q_refs[g], k_refs[g], kp_refs[g]
        span = C_BLOCK * dil
        nblk = tt // span
        def rows(start, dil=dil):
            return pl.ds(start, C_BLOCK, stride=dil) if dil > 1 else pl.ds(start, C_BLOCK)

        blocks = [(rows(r + n * span), rows(r + (n - 1) * span) if n else rows(r + (nblk - 1) * span), n == 0)
                  for r in range(dil) for n in range(nblk)]

        for j, (sl, psl, from_prev_tile) in enumerate(blocks):
            q = (q_ref[sl, :] * scale).astype(BF16)
            kp = (kp_ref if from_prev_tile else k_ref)[psl, :].astype(BF16)
            s_p = _dot_nt(q, kp) + bias_ref[g, 0, :, lo]
            if from_prev_tile:
                s_p = jnp.where(first, -jnp.inf, s_p)
            s_ref[j, :, lo] = s_p
            s_ref[j, :, hi] = _dot_nt(q, k_ref[sl, :].astype(BF16)) + bias_ref[g, 0, :, hi]

        for j, (sl, psl, from_prev_tile) in enumerate(blocks):
            s = s_ref[j]
            mx = jnp.max(s, axis=-1, keepdims=True)
            p = jnp.exp(s - mx)
            den = jnp.sum(p, axis=-1, keepdims=True)
            p_ref[j] = p.astype(BF16)
            inv_ref[j] = jnp.broadcast_to(1.0 / den, shape)
            lse_ref[g, sl, :] = jnp.broadcast_to(mx + jnp.log(den), shape)

        for j, (sl, psl, from_prev_tile) in enumerate(blocks):
            vp = (vp_ref if from_prev_tile else v_ref)[psl, :].astype(BF16)
            num = _dot(p_ref[j, :, lo], vp) + _dot(p_ref[j, :, hi], v_ref[sl, :].astype(BF16))
            og_ref[g, sl, :] = num * inv_ref[j]

    piece = 2 * C_BLOCK
    for i in range(tt // piece):
        sl = slice(i * piece, (i + 1) * piece)
        l0, l1, l2 = lse_ref[0, sl, :], lse_ref[1, sl, :], lse_ref[2, sl, :]
        lm = jnp.maximum(jnp.maximum(l0, l1), l2)
        e0, e1, e2 = jnp.exp(l0 - lm), jnp.exp(l1 - lm), jnp.exp(l2 - lm)
        num = e0 * og_ref[0, sl, :] + e1 * og_ref[1, sl, :] + e2 * og_ref[2, sl, :]
        o_ref[sl, :] = (num / (e0 + e1 + e2)).astype(o_ref.dtype)


def _dilated_attn(cproj, rel_bias, batch, seq):
    tt = TT_ATTN
    assert seq % tt == 0 and tt == C_BLOCK * max(d for _, d in C_PATTERNS)
    nt = seq // tt
    biases = []
    for gi, (window, dil) in enumerate(C_PATTERNS):
        band, bucket = _band_pattern(window, dil)
        onehot = jnp.asarray(bucket[..., None] == np.arange(REL_BUCKETS), F32)
        table = rel_bias[:, gi * C_HEADS:(gi + 1) * C_HEADS].astype(F32)
        bias = jnp.einsum('qkn,nh->hqk', onehot, table, precision=lax.Precision.HIGHEST)
        biases.append(jnp.where(band[None], bias, -jnp.inf))
    bias = jnp.stack(biases)

    def cur(off, g=0):
        c0 = off // C_HEAD_DIM + g * C_HEADS
        return pl.BlockSpec((tt, C_HEAD_DIM), lambda b, t, h: (b * nt + t, c0 + h))

    def prev(off, g=0):
        c0 = off // C_HEAD_DIM + g * C_HEADS
        return pl.BlockSpec((tt, C_HEAD_DIM), lambda b, t, h: (b * nt + jnp.maximum(t - 1, 0), c0 + h))

    ng = len(C_PATTERNS)
    nblocks = tt // C_BLOCK
    in_specs = ([cur(CQ_OFF, g) for g in range(ng)] + [cur(CK_OFF, g) for g in range(ng)]
                + [prev(CK_OFF, g) for g in range(ng)] + [cur(CV_OFF), prev(CV_OFF)]
                + [pl.BlockSpec((ng, 1, C_BLOCK, 2 * C_BLOCK), lambda b, t, h: (0, h, 0, 0))])
    return pl.pallas_call(
        _attn_kernel,
        grid=(batch, nt, C_HEADS),
        in_specs=in_specs,
        out_specs=pl.BlockSpec((tt, C_HEAD_DIM), lambda b, t, h: (b * nt + t, h)),
        out_shape=jax.ShapeDtypeStruct((batch * seq, C_WIDTH), BF16),
        scratch_shapes=[pltpu.VMEM((nblocks, C_BLOCK, 2 * C_BLOCK), F32),
                        pltpu.VMEM((nblocks, C_BLOCK, 2 * C_BLOCK), BF16),
                        pltpu.VMEM((nblocks, C_BLOCK, C_HEAD_DIM), F32),
                        pltpu.VMEM((ng, tt, C_HEAD_DIM), F32),
                        pltpu.VMEM((ng, tt, C_HEAD_DIM), F32)],
        compiler_params=_cparams("parallel", "arbitrary", "arbitrary"),
        name="dilated_attn",
    )(*([cproj] * (3 * ng + 2)), bias)


def _merge_kernel(x_ref, g0_ref, g1_ref, g2_ref, ya_ref, yb_ref, yc_ref, wb_ref, wo_ref, out_ref):
    merged = (_sigmoid(g0_ref[...].astype(F32)) * _dot(ya_ref[...], wb_ref[0])
              + _sigmoid(g1_ref[...].astype(F32)) * _dot(yb_ref[...], wb_ref[1])
              + _sigmoid(g2_ref[...].astype(F32)) * _dot(yc_ref[...], wb_ref[2]))
    out_ref[...] = x_ref[...] + _dot(merged.astype(BF16), wo_ref[...])


def _merge(x2d, proj, ya, yb, yc, w_branch, w_out):
    n, d = x2d.shape
    tm = min(TM_MERGE, n)
    row = lambda w: pl.BlockSpec((tm, w), lambda i: (i, 0))
    gate = lambda g: pl.BlockSpec((tm, d), lambda i: (i, GATE_OFF // d + g))
    return pl.pallas_call(
        _merge_kernel,
        grid=(n // tm,),
        in_specs=[row(d), gate(0), gate(1), gate(2), row(WBLK), row(WBLK), row(WBLK),
                  pl.BlockSpec((3, WBLK, d), lambda i: (0, 0, 0)),
                  pl.BlockSpec((d, d), lambda i: (0, 0))],
        out_specs=row(d),
        out_shape=jax.ShapeDtypeStruct((n, d), F32),
        compiler_params=_cparams("parallel"),
        name="merge",
    )(x2d, proj, proj, proj, ya, yb, yc, w_branch.astype(BF16), w_out.astype(BF16))


EXP_LANE0 = MOE_GROUPS
META_E, META_RANK, META_GATE = 0, 2, 4


def _split3_dot(h, w_hi, w_lo):
    h_hi = h.astype(BF16)
    h_lo = (h - h_hi.astype(F32)).astype(BF16)
    return _dot(h_hi, w_hi) + (_dot(h_lo, w_hi) + _dot(h_hi, w_lo))


def _router_kernel(x_ref, g_ref, whi_ref, wlo_ref, meta_ref, cnt_ref, run_ref):
    tm = x_ref.shape[0]

    @pl.when(pl.program_id(0) == 0)
    def _():
        run_ref[...] = jnp.zeros_like(run_ref)

    x = x_ref[...]
    h = x * lax.rsqrt(jnp.mean(x * x, axis=-1, keepdims=True) + EPS) * g_ref[...]
    logits = _split3_dot(h, whi_ref[...], wlo_ref[...])
    lane = lax.broadcasted_iota(jnp.int32, (tm, LANES), 1)
    lane_f = lane.astype(F32)
    big = float(LANES)

    def first_argmax(vals, mx):
        return jnp.min(jnp.where(vals == mx, lane_f, big), axis=-1, keepdims=True)

    gl = jnp.where(lane < MOE_GROUPS, logits, -jnp.inf)
    gmax = jnp.max(gl, axis=-1, keepdims=True)
    grp_p = 1.0 / jnp.sum(jnp.exp(gl - gmax), axis=-1, keepdims=True)
    gidx = first_argmax(gl, gmax)
    lo = EXP_LANE0 + gidx * MOE_PER_GROUP
    in_grp = (lane_f >= lo) & (lane_f < lo + MOE_PER_GROUP)
    el = jnp.where(in_grp, logits, -jnp.inf)
    m1 = jnp.max(el, axis=-1, keepdims=True)
    i1 = first_argmax(el, m1)
    el2 = jnp.where(lane_f == i1, -jnp.inf, el)
    m2 = jnp.max(el2, axis=-1, keepdims=True)
    i2 = first_argmax(el2, m2)
    t2 = jnp.exp(m2 - m1)
    gate1 = grp_p / (1.0 + t2)
    gate2 = grp_p * t2 / (1.0 + t2)
    oh1 = lane_f == i1
    oh2 = lane_f == i2
    onehot = jnp.where(oh1 | oh2, 1.0, 0.0)
    ri = lax.broadcasted_iota(jnp.int32, (tm, tm), 0)
    ci = lax.broadcasted_iota(jnp.int32, (tm, tm), 1)
    before = jnp.where(ri > ci, 1.0, 0.0).astype(BF16)
    prior = _dot(before, onehot.astype(BF16)) + run_ref[0:1, :]
    rank1 = jnp.sum(jnp.where(oh1, prior, 0.0), axis=-1, keepdims=True)
    rank2 = jnp.sum(jnp.where(oh2, prior, 0.0), axis=-1, keepdims=True)
    run_new = run_ref[0:1, :] + jnp.sum(onehot, axis=0, keepdims=True)
    run_ref[0:1, :] = run_new
    cnt_ref[...] = jnp.broadcast_to(run_new, cnt_ref.shape)
    rec = jnp.zeros((tm, LANES), F32)
    for ln, val in ((META_E, i1 - EXP_LANE0), (META_E + 1, i2 - EXP_LANE0), (META_RANK, rank1),
                    (META_RANK + 1, rank2), (META_GATE, gate1), (META_GATE + 1, gate2)):
        rec = jnp.where(lane == ln, val, rec)
    meta_ref[...] = rec


def _router(x2d, g, w_group, w_expert):
    n, d = x2d.shape
    tm = min(TM_ROUTER, n)
    w = jnp.concatenate([w_group, w_expert], axis=1).astype(F32)
    w = jnp.pad(w, ((0, 0), (0, LANES - w.shape[1])))
    w_hi = w.astype(BF16)
    w_lo = (w - w_hi.astype(F32)).astype(BF16)
    return pl.pallas_call(
        _router_kernel,
        grid=(n // tm,),
        in_specs=[pl.BlockSpec((tm, d), lambda i: (i, 0)),
                  pl.BlockSpec((1, d), lambda i: (0, 0)),
                  pl.BlockSpec((d, LANES), lambda i: (0, 0)),
                  pl.BlockSpec((d, LANES), lambda i: (0, 0))],
        out_specs=[pl.BlockSpec((tm, LANES), lambda i: (i, 0)),
                   pl.BlockSpec((SUBLANES, LANES), lambda i: (0, 0))],
        out_shape=[jax.ShapeDtypeStruct((n, LANES), F32),
                   jax.ShapeDtypeStruct((SUBLANES, LANES), F32)],
        scratch_shapes=[pltpu.VMEM((SUBLANES, LANES), F32)],
        compiler_params=_cparams("arbitrary"),
        name="router",
    )(x2d, g.reshape(1, d), w_hi, w_lo)


DMA_UNROLL = 8


ROW_SUB = D_MODEL // LANES
assert ROW_SUB == SUBLANES


def _row_tile_store(ref, y):
    m = y.shape[0]
    for s in range(ROW_SUB):
        ref[pl.ds(s, m, stride=ROW_SUB), :] = y[:, s * LANES:(s + 1) * LANES]


def _row_tile_load(ref, m, s):
    return ref[pl.ds(s, m, stride=ROW_SUB), :]


def _row_of(ref, i):
    return ref.at[pl.ds(pl.multiple_of(i * ROW_SUB, ROW_SUB), ROW_SUB)]


def _start_rows(row_copy, m):
    def body(j, c):
        row_copy(j, 0).start(priority=0)
        row_copy(j, 1).start(priority=1)
        return c
    lax.fori_loop(0, m, body, 0, unroll=DMA_UNROLL)


def _wait_rows(row_copy, m):
    def body(j, c):
        row_copy(j, 0).wait()
        row_copy(j, 1).wait()
        return c
    lax.fori_loop(0, m, body, 0, unroll=DMA_UNROLL)


def _dispatch_kernel(dest_ref, x_ref, g_ref, xs_init_hbm, xs_hbm, xta_ref, xtb_ref, sem_a, sem_b):
    del xs_init_hbm
    th = x_ref.shape[0] // 2
    i = pl.program_id(0)

    def normed(x):
        return x * lax.rsqrt(jnp.mean(x * x, axis=-1, keepdims=True) + EPS) * g_ref[...]

    def copy_a(j, k):
        return pltpu.make_async_copy(_row_of(xta_ref, j), xs_hbm.at[dest_ref[2 * j + k]], sem_a)

    def copy_b(j, k):
        return pltpu.make_async_copy(_row_of(xtb_ref, j), xs_hbm.at[dest_ref[2 * th + 2 * j + k]], sem_b)

    @pl.when(i > 0)
    def _():
        _wait_rows(copy_a, th)
    _row_tile_store(xta_ref, normed(x_ref[:th, :]))
    _start_rows(copy_a, th)

    @pl.when(i > 0)
    def _():
        _wait_rows(copy_b, th)
    _row_tile_store(xtb_ref, normed(x_ref[th:, :]))
    _start_rows(copy_b, th)

    @pl.when(i == pl.num_programs(0) - 1)
    def _():
        _wait_rows(copy_a, th)
        _wait_rows(copy_b, th)


def _dispatch(x2d, g, dest, rows):
    n, d = x2d.shape
    td = min(TD_DISPATCH, n)
    return pl.pallas_call(
        _dispatch_kernel,
        grid=(n // td,),
        in_specs=[pl.BlockSpec((2 * td,), lambda i: (i,), memory_space=pltpu.SMEM),
                  pl.BlockSpec((td, d), lambda i: (i, 0)),
                  pl.BlockSpec((1, d), lambda i: (0, 0)),
                  pl.BlockSpec(memory_space=pl.ANY)],
        out_specs=pl.BlockSpec(memory_space=pl.ANY),
        out_shape=jax.ShapeDtypeStruct((rows, ROW_SUB, LANES), F32),
        scratch_shapes=[pltpu.VMEM((td // 2 * ROW_SUB, LANES), F32), pltpu.VMEM((td // 2 * ROW_SUB, LANES), F32),
                        pltpu.SemaphoreType.DMA, pltpu.SemaphoreType.DMA],
        input_output_aliases={3: 0},
        compiler_params=_cparams("arbitrary"),
        name="moe_dispatch",
    )(dest, x2d, g.reshape(1, d), jnp.zeros((rows, ROW_SUB, LANES), F32))


def _experts_kernel(be_ref, nused_ref, xs_ref, w1_ref, w3_ref, w2_ref, ys_ref, w1b_ref, w3b_ref, w2b_ref):
    i = pl.program_id(0)
    bm = xs_ref.shape[0] // ROW_SUB

    @pl.when(jnp.logical_or(i == 0, be_ref[i] != be_ref[jnp.maximum(i - 1, 0)]))
    def _():
        w1b_ref[...] = w1_ref[0].astype(BF16)
        w3b_ref[...] = w3_ref[0].astype(BF16)
        w2b_ref[...] = w2_ref[0].astype(BF16)

    @pl.when(i < nused_ref[0])
    def _():
        h = jnp.concatenate([_row_tile_load(xs_ref, bm, s) for s in range(ROW_SUB)], axis=1).astype(BF16)
        hid = _silu(_dot(h, w1b_ref[...])) * _dot(h, w3b_ref[...])
        _row_tile_store(ys_ref, _dot(hid.astype(BF16), w2b_ref[...]))

    @pl.when(i >= nused_ref[0])
    def _():
        ys_ref[...] = jnp.zeros_like(ys_ref)


def _experts(xs, blk_expert, n_used, w1, w3, w2):
    rows = xs.shape[0] // ROW_SUB
    d = D_MODEL
    nb = rows // BM_EXPERT
    ff = w1.shape[-1]
    blk = BM_EXPERT * ROW_SUB
    grid_spec = pltpu.PrefetchScalarGridSpec(
        num_scalar_prefetch=2,
        grid=(nb,),
        in_specs=[pl.BlockSpec((blk, LANES), lambda i, be, nu: (i, 0)),
                  pl.BlockSpec((1, d, ff), lambda i, be, nu: (be[i], 0, 0)),
                  pl.BlockSpec((1, d, ff), lambda i, be, nu: (be[i], 0, 0)),
                  pl.BlockSpec((1, ff, d), lambda i, be, nu: (be[i], 0, 0))],
        out_specs=pl.BlockSpec((blk, LANES), lambda i, be, nu: (i, 0)),
        scratch_shapes=[pltpu.VMEM((d, ff), BF16), pltpu.VMEM((d, ff), BF16), pltpu.VMEM((ff, d), BF16)],
    )
    return pl.pallas_call(
        _experts_kernel,
        grid_spec=grid_spec,
        out_shape=jax.ShapeDtypeStruct((rows * ROW_SUB, LANES), F32),
        compiler_params=_cparams("arbitrary"),
        name="moe_experts",
    )(blk_expert, n_used, xs, w1, w3, w2)


def _combine_kernel(final, dest_ref, dest_next_ref, x_ref, meta_ref, ys_hbm, g_ref, o_ref,
                    ya0_ref, ya1_ref, yb0_ref, yb1_ref, sem_a, sem_b):
    th = x_ref.shape[0] // 2
    i = pl.program_id(0)

    def gather(dref, off, bufs, sem):
        def row_copy(j, k):
            return pltpu.make_async_copy(ys_hbm.at[dref[off + 2 * j + k]], _row_of(bufs[k], j), sem)
        return row_copy

    copy_a = gather(dest_ref, 0, (ya0_ref, ya1_ref), sem_a)
    copy_b = gather(dest_ref, 2 * th, (yb0_ref, yb1_ref), sem_b)
    copy_a_next = gather(dest_next_ref, 0, (ya0_ref, ya1_ref), sem_a)

    def combine(half, y0_ref, y1_ref):
        rows = slice(half * th, (half + 1) * th)
        meta = meta_ref[rows, :]
        gate0 = meta[:, META_GATE:META_GATE + 1]
        gate1 = meta[:, META_GATE + 1:META_GATE + 2]
        cols = [slice(s * LANES, (s + 1) * LANES) for s in range(ROW_SUB)]
        ys = [x_ref[rows, cols[s]] + gate0 * _row_tile_load(y0_ref, th, s) + gate1 * _row_tile_load(y1_ref, th, s)
              for s in range(ROW_SUB)]
        if final:
            ssq = sum(jnp.sum(y * y, axis=-1, keepdims=True) for y in ys)
            inv = lax.rsqrt(ssq * (1.0 / D_MODEL) + EPS)
            ys = [y * inv * g_ref[:, cols[s]] for s, y in enumerate(ys)]
        for s, y in enumerate(ys):
            o_ref[rows, cols[s]] = y

    @pl.when(i == 0)
    def _():
        _start_rows(copy_a, th)
    _start_rows(copy_b, th)
    _wait_rows(copy_a, th)
    combine(0, ya0_ref, ya1_ref)

    @pl.when(i + 1 < pl.num_programs(0))
    def _():
        _start_rows(copy_a_next, th)
    _wait_rows(copy_b, th)
    combine(1, yb0_ref, yb1_ref)


def _combine(x2d, meta, dest, ys, final_g, final):
    n, d = x2d.shape
    tc = min(TC_COMBINE, n)
    nsteps = n // tc
    half_buf = pltpu.VMEM((tc // 2 * ROW_SUB, LANES), F32)
    return pl.pallas_call(
        functools.partial(_combine_kernel, final),
        grid=(nsteps,),
        in_specs=[pl.BlockSpec((2 * tc,), lambda i: (i,), memory_space=pltpu.SMEM),
                  pl.BlockSpec((2 * tc,), lambda i: (jnp.minimum(i + 1, nsteps - 1),), memory_space=pltpu.SMEM),
                  pl.BlockSpec((tc, d), lambda i: (i, 0)),
                  pl.BlockSpec((tc, LANES), lambda i: (i, 0)),
                  pl.BlockSpec(memory_space=pl.ANY),
                  pl.BlockSpec((1, d), lambda i: (0, 0))],
        out_specs=pl.BlockSpec((tc, d), lambda i: (i, 0)),
        out_shape=jax.ShapeDtypeStruct((n, d), F32),
        scratch_shapes=[half_buf, half_buf, half_buf, half_buf,
                        pltpu.SemaphoreType.DMA, pltpu.SemaphoreType.DMA],
        compiler_params=_cparams("arbitrary"),
        name="moe_combine_final" if final else "moe_combine",
    )(dest, dest, x2d, meta, ys, final_g.reshape(1, d))


def _moe(x2d, norm_g, w_group, w_expert, w1, w3, w2, layer, final_g, final):
    n, d = x2d.shape
    meta, cnt = _router(x2d, norm_g, w_group, w_expert)
    counts = cnt[0, EXP_LANE0:EXP_LANE0 + MOE_EXPERTS].astype(jnp.int32)
    padded = (counts + BM_EXPERT - 1) // BM_EXPERT * BM_EXPERT
    pad_end = jnp.cumsum(padded)
    pad_start = pad_end - padded
    e = meta[:, META_E:META_E + 2].astype(jnp.int32)
    rank = meta[:, META_RANK:META_RANK + 2].astype(jnp.int32)
    eids = jnp.arange(MOE_EXPERTS, dtype=jnp.int32)
    dest = (jnp.sum(jnp.where(e[..., None] == eids, pad_start, 0), axis=-1) + rank).reshape(2 * n)
    rows = 2 * n + MOE_EXPERTS * BM_EXPERT
    nb = rows // BM_EXPERT
    blk_start = jnp.arange(nb, dtype=jnp.int32) * BM_EXPERT
    blk_expert = jnp.minimum(jnp.sum((pad_end[None, :] <= blk_start[:, None]).astype(jnp.int32), axis=1),
                             MOE_EXPERTS - 1)
    n_used = (pad_end[-1:] // BM_EXPERT).astype(jnp.int32)
    xs = _dispatch(x2d, norm_g, dest, rows)
    ys = _experts(xs.reshape(rows * ROW_SUB, LANES), blk_expert + layer * MOE_EXPERTS, n_used, w1, w3, w2)
    return _combine(x2d, meta, dest, ys.reshape(rows, ROW_SUB, LANES), final_g, final)


def _split_w_in(w):
    n_ab = 2 * A_WIDTH + 4 * B_WIDTH
    n_c = n_ab + 2 * B_HEADS
    n_g = n_c + 2 * 3 * C_WIDTH + C_WIDTH
    w_main = jnp.concatenate([w[:, n_g:], w[:, :n_ab]], axis=1).astype(BF16)
    w_c = w[:, n_c:n_g].astype(BF16)
    w_small = jnp.pad(w[:, n_ab:n_c], ((0, 0), (0, LANES - 2 * B_HEADS))).astype(BF16)
    return w_main, w_c, w_small


def kernel(x, norm_mix_g, w_in, a_ln_g, a_ln_b, a_w_s, a_b_s, b_conv_w, b_a_log, b_dt_bias, b_norm_g, rel_bias,
           w_branch, w_out, norm_ffn_g, w_group, w_expert, w1, w3, w2, final_norm_g):
    batch, seq, d = x.shape
    depth = w_in.shape[0]
    x2d = x.reshape(batch * seq, d)
    for l in range(depth):
        w_main, w_c, w_small = _split_w_in(w_in[l])
        assert w_main.shape[1] == PROJ_W and w_c.shape[1] == CPROJ_W
        proj = _rms_matmul(x2d, norm_mix_g[l], w_main, BF16, TN_PROJ_MAIN)
        cproj = _rms_matmul(x2d, norm_mix_g[l], w_c, F32, TN_PROJ)
        p2 = _rms_matmul(x2d, norm_mix_g[l], w_small, F32, LANES)
        ya = _gmlp(proj, a_w_s[l], a_b_s[l], a_ln_g[l], a_ln_b[l])
        yb = _deltanet(proj, p2, b_conv_w[l], b_a_log[l], b_dt_bias[l], b_norm_g[l], batch, seq)
        yc = _dilated_attn(cproj, rel_bias, batch, seq)
        x2d = _merge(x2d, proj, ya, yb, yc, w_branch[l], w_out[l])
        x2d = _moe(x2d, norm_ffn_g[l], w_group[l], w_expert[l], w1.reshape((-1,) + w1.shape[2:]),
                   w3.reshape((-1,) + w3.shape[2:]), w2.reshape((-1,) + w2.shape[2:]), l, final_norm_g,
                   final=(l == depth - 1))
    return x2d.reshape(batch, seq, d)
```

```python
import functools
import math

import numpy as np
import jax
import jax.numpy as jnp
from jax import lax
from jax.experimental import pallas as pl
from jax.experimental.pallas import tpu as pltpu

F32 = jnp.float32
BF16 = jnp.bfloat16

D_MODEL = 1024
A_GROUPS, A_CHUNK, A_WIDTH = 4, 128, 512
B_HEADS, B_HEAD_DIM, B_CONV, B_WIDTH = 4, 128, 4, 512
C_PATTERNS = ((128, 1), (512, 4), (2048, 16))
C_HEADS, C_HEAD_DIM, C_BLOCK, C_WIDTH = 4, 128, 128, 512
REL_BUCKETS, REL_MAX_DIST = 32, 2048
MOE_GROUPS, MOE_PER_GROUP, MOE_EXPERTS, MOE_FF = 4, 8, 32, 512
EPS = 1e-6

LANES = 128
SUBLANES = 8
VMEM_LIMIT = 48 * 1024 * 1024

PROJ_W = 6144
GATE_OFF, AU_OFF, AV_OFF = 0, 3072, 3584
BQ_OFF, BK_OFF, BV_OFF, BZ_OFF = 4096, 4608, 5120, 5632
WBLK = 512
CPROJ_W = 3584
CQ_OFF, CK_OFF, CV_OFF = 0, 1536, 3072

TM_PROJ, TN_PROJ = 2048, 512
TB_DELTA = 512
CHUNK = 128
TT_ATTN = 2048
TM_MERGE = 512
TM_ROUTER = 512
TD_DISPATCH = 512
BM_EXPERT = 512
TC_COMBINE = 512


def _cparams(*sem):
    return pltpu.CompilerParams(dimension_semantics=sem, vmem_limit_bytes=VMEM_LIMIT)


def _dot(a, b):
    return jnp.dot(a, b, preferred_element_type=F32)


def _dot_nt(a, b):
    return lax.dot_general(a, b, (((1,), (1,)), ((), ())), preferred_element_type=F32)


def _sigmoid(x):
    return 1.0 / (1.0 + jnp.exp(-x))


def _silu(x):
    return x * _sigmoid(x)


def _gelu_tanh(x):
    c = math.sqrt(2.0 / math.pi)
    return x * (0.5 * (1.0 + jnp.tanh(c * (x + 0.044715 * (x * x * x)))))


def _softplus(x):
    return jnp.maximum(x, 0.0) + jnp.log(1.0 + jnp.exp(-jnp.abs(x)))


def _in_proj_kernel(nm, nc, x_ref, g_ref, wm_ref, wc_ref, ws_ref, proj_ref, cproj_ref, p2_ref, h_ref):
    j = pl.program_id(1)

    @pl.when(j == 0)
    def _():
        x = x_ref[...]
        ms = jnp.mean(x * x, axis=-1, keepdims=True)
        h_ref[...] = (x * lax.rsqrt(ms + EPS) * g_ref[...]).astype(h_ref.dtype)

    @pl.when(j < nm)
    def _():
        proj_ref[...] = _dot(h_ref[...], wm_ref[...]).astype(proj_ref.dtype)

    @pl.when(jnp.logical_and(j >= nm, j < nm + nc))
    def _():
        cproj_ref[...] = _dot(h_ref[...], wc_ref[...])

    @pl.when(j == nm + nc)
    def _():
        p2_ref[...] = _dot(h_ref[...], ws_ref[...])


def _in_proj(x2d, g, w_main, w_c, w_small):
    n, d = x2d.shape
    tm = min(TM_PROJ, n)
    tn = tn_main = TN_PROJ
    nm, nc = w_main.shape[1] // tn_main, w_c.shape[1] // tn
    main_col = lambda i, j: jnp.minimum(j, nm - 1)
    c_col = lambda i, j: jnp.clip(j - nm, 0, nc - 1)
    return pl.pallas_call(
        functools.partial(_in_proj_kernel, nm, nc),
        grid=(n // tm, nm + nc + 1),
        in_specs=[pl.BlockSpec((tm, d), lambda i, j: (i, 0)),
                  pl.BlockSpec((1, d), lambda i, j: (0, 0)),
                  pl.BlockSpec((d, tn_main), lambda i, j: (0, main_col(i, j))),
                  pl.BlockSpec((d, tn), lambda i, j: (0, c_col(i, j))),
                  pl.BlockSpec((d, LANES), lambda i, j: (0, 0))],
        out_specs=[pl.BlockSpec((tm, tn_main), lambda i, j: (i, main_col(i, j))),
                   pl.BlockSpec((tm, tn), lambda i, j: (i, c_col(i, j))),
                   pl.BlockSpec((tm, LANES), lambda i, j: (i, 0))],
        out_shape=[jax.ShapeDtypeStruct((n, w_main.shape[1]), BF16),
                   jax.ShapeDtypeStruct((n, w_c.shape[1]), F32),
                   jax.ShapeDtypeStruct((n, LANES), F32)],
        scratch_shapes=[pltpu.VMEM((tm, d), BF16)],
        compiler_params=_cparams("parallel", "arbitrary"),
        name="in_proj",
    )(x2d, g.reshape(1, d), w_main, w_c, w_small)


def _gmlp_kernel(u_ref, v_ref, w_ref, b_ref, lng_ref, lnb_ref, o_ref):
    tm = u_ref.shape[0]
    u = _gelu_tanh(u_ref[...].astype(F32))
    v = _gelu_tanh(v_ref[...].astype(F32))
    mu = jnp.mean(v, axis=-1, keepdims=True)
    vc = v - mu
    var = jnp.mean(vc * vc, axis=-1, keepdims=True)
    vn = (vc * lax.rsqrt(var + EPS) * lng_ref[...] + lnb_ref[...]).astype(BF16)
    for c in range(tm // A_CHUNK):
        rows = slice(c * A_CHUNK, (c + 1) * A_CHUNK)
        for g in range(A_GROUPS):
            cols = slice(g * LANES, (g + 1) * LANES)
            sv = _dot(w_ref[g], vn[rows, cols]) + b_ref[:, cols]
            o_ref[rows, cols] = (u[rows, cols] * sv).astype(o_ref.dtype)


def _deltanet_kernel(q_ref, k_ref, v_ref, z_ref, p2_ref, cw_ref, alog_ref, dtb_ref, ng_ref, o_ref,
                     s_ref, carry_ref, qkv_ref, gc_ref, gct_ref, beta_ref, r_ref, p_ref, stack_ref, attn_ref,
                     rhs_ref, rhsb_ref, u_ref, wq_ref, kt_ref, eg_ref, vn_ref, op_ref):
    tb = q_ref.shape[0]
    nchunk = tb // CHUNK

    @pl.when(pl.program_id(1) == 0)
    def _():
        s_ref[...] = jnp.zeros_like(s_ref)
        carry_ref[...] = jnp.zeros_like(carry_ref)

    row8 = lax.broadcasted_iota(jnp.int32, (SUBLANES, B_WIDTH), 0)
    for idx, ref in enumerate((q_ref, k_ref, v_ref)):
        x = ref[...].astype(F32)
        w = cw_ref[:, idx * B_WIDTH:(idx + 1) * B_WIDTH]
        prev = carry_ref[idx]
        acc = x * w[B_CONV - 1:B_CONV]
        for s in range(1, B_CONV):
            xs = pltpu.roll(x, s, axis=0)
            ps = pltpu.roll(prev, s, axis=0)
            head = jnp.where(row8 < s, ps, xs[:SUBLANES])
            xs = jnp.concatenate([head, xs[SUBLANES:]], axis=0)
            acc = acc + xs * w[B_CONV - 1 - s:B_CONV - s]
        carry_ref[idx] = x[tb - SUBLANES:]
        qkv_ref[idx] = _silu(acc)

    ri = lax.broadcasted_iota(jnp.int32, (CHUNK, CHUNK), 0)
    ci = lax.broadcasted_iota(jnp.int32, (CHUNK, CHUNK), 1)
    incl = ri >= ci
    strict = ri > ci
    tril_f = incl.astype(F32)
    alog = alog_ref[...]
    dtb = dtb_ref[...]
    ng = ng_ref[...]
    lo, hi = slice(0, CHUNK), slice(CHUNK, 2 * CHUNK)
    items = [(c, h) for c in range(nchunk) for h in range(B_HEADS)]

    for c in range(nchunk):
        rows = slice(c * CHUNK, (c + 1) * CHUNK)
        p2 = p2_ref[rows, :]
        g_all = -jnp.exp(alog) * _softplus(p2 + dtb)
        gc_all = jnp.dot(tril_f, g_all, preferred_element_type=F32, precision=lax.Precision.HIGHEST)
        gc_ref[c] = gc_all
        gct_ref[c] = gc_all.T
        beta_ref[c] = _sigmoid(p2)

    for i, (c, h) in enumerate(items):
        rows = slice(c * CHUNK, (c + 1) * CHUNK)
        cols = slice(h * B_HEAD_DIM, (h + 1) * B_HEAD_DIM)
        q = qkv_ref[0, rows, cols]
        k = qkv_ref[1, rows, cols]
        v = qkv_ref[2, rows, cols]
        q = q * (lax.rsqrt(jnp.sum(q * q, axis=-1, keepdims=True) + EPS) * (B_HEAD_DIM ** -0.5))
        k = k * lax.rsqrt(jnp.sum(k * k, axis=-1, keepdims=True) + EPS)
        beta = beta_ref[c, :, h:h + 1]
        gc_col = jnp.broadcast_to(gc_ref[c, :, B_HEADS + h:B_HEADS + h + 1], (CHUNK, B_HEAD_DIM))
        gc_row = gct_ref[c, B_HEADS + h:B_HEADS + h + 1, :]
        decay = jnp.exp(jnp.where(incl, gc_col - gc_row, -jnp.inf))
        kb = k * beta
        vb = v * beta
        egc = jnp.exp(gc_col)
        aq = _dot_nt(jnp.concatenate([kb, q], axis=0).astype(BF16), k.astype(BF16))
        m = -jnp.where(strict, aq[lo] * decay, 0.0)
        r_ref[i] = m
        stack_ref[i, lo] = m.astype(BF16)
        attn_ref[i] = (aq[hi] * decay).astype(BF16)
        rhs = jnp.concatenate([vb, kb * egc], axis=1)
        rhs_ref[i] = rhs
        rhsb_ref[i] = rhs.astype(BF16)
        wq_ref[i, hi] = (q * egc).astype(BF16)
        g_last = gc_col[CHUNK - 1:CHUNK, :]
        kt_ref[i] = (k * jnp.exp(g_last - gc_col)).T.astype(BF16)
        eg_ref[i] = jnp.exp(g_last)

    for i in range(len(items)):
        mb = stack_ref[i, lo]
        p = _dot(mb, mb)
        p_ref[i] = p
        stack_ref[i, hi] = p.astype(BF16)
    for _ in range(5):
        for i in range(len(items)):
            rp = _dot(stack_ref[i], stack_ref[i, hi])
            r = r_ref[i] + p_ref[i] + rp[lo]
            p = rp[hi]
            r_ref[i] = r
            p_ref[i] = p
            stack_ref[i, lo] = r.astype(BF16)
            stack_ref[i, hi] = p.astype(BF16)
    for i in range(len(items)):
        r = r_ref[i] + p_ref[i] + _dot(stack_ref[i, lo], stack_ref[i, hi])
        sol = rhs_ref[i] + _dot(r.astype(BF16), rhsb_ref[i])
        u_ref[i] = sol[:, :B_HEAD_DIM]
        wq_ref[i, lo] = sol[:, B_HEAD_DIM:].astype(BF16)

    for c in range(nchunk):
        rows = slice(c * CHUNK, (c + 1) * CHUNK)
        for h in range(B_HEADS):
            i = c * B_HEADS + h
            wq = _dot(wq_ref[i], s_ref[h].astype(BF16))
            vn_ref[h] = (u_ref[i] - wq[lo]).astype(BF16)
            op_ref[h] = wq[hi]
        for h in range(B_HEADS):
            i = c * B_HEADS + h
            cols = slice(h * B_HEAD_DIM, (h + 1) * B_HEAD_DIM)
            vnb = vn_ref[h]
            o = op_ref[h] + _dot(attn_ref[i], vnb)
            s_ref[h] = s_ref[h] * eg_ref[i] + _dot(kt_ref[i], vnb)
            z = z_ref[rows, cols].astype(F32)
            on = o * lax.rsqrt(jnp.mean(o * o, axis=-1, keepdims=True) + EPS) * ng
            o_ref[rows, cols] = (on * _silu(z)).astype(o_ref.dtype)


def _deltanet(proj, p2, conv_w, a_log, dt_bias, norm_g, batch, seq):
    tb = min(TB_DELTA, seq)
    nt = seq // tb
    nchunk = tb // CHUNK
    ni = nchunk * B_HEADS
    dk = B_HEAD_DIM

    def col(off):
        return pl.BlockSpec((tb, WBLK), lambda b, t: (b * nt + t, off // WBLK))

    pad = LANES - 2 * B_HEADS
    alog = jnp.pad(a_log.astype(F32), (B_HEADS, pad)).reshape(1, LANES)
    dtb = jnp.pad(dt_bias.astype(F32), (B_HEADS, pad)).reshape(1, LANES)
    return pl.pallas_call(
        _deltanet_kernel,
        grid=(batch, nt),
        in_specs=[col(BQ_OFF), col(BK_OFF), col(BV_OFF), col(BZ_OFF),
                  pl.BlockSpec((tb, LANES), lambda b, t: (b * nt + t, 0)),
                  pl.BlockSpec((B_CONV, 3 * B_WIDTH), lambda b, t: (0, 0)),
                  pl.BlockSpec((1, LANES), lambda b, t: (0, 0)),
                  pl.BlockSpec((1, LANES), lambda b, t: (0, 0)),
                  pl.BlockSpec((1, B_HEAD_DIM), lambda b, t: (0, 0))],
        out_specs=pl.BlockSpec((tb, B_WIDTH), lambda b, t: (b * nt + t, 0)),
        out_shape=jax.ShapeDtypeStruct((batch * seq, B_WIDTH), BF16),
        scratch_shapes=[pltpu.VMEM((B_HEADS, dk, dk), F32),
                        pltpu.VMEM((3, SUBLANES, B_WIDTH), F32),
                        pltpu.VMEM((3, tb, B_WIDTH), F32),
                        pltpu.VMEM((nchunk, CHUNK, LANES), F32),
                        pltpu.VMEM((nchunk, LANES, CHUNK), F32),
                        pltpu.VMEM((nchunk, CHUNK, LANES), F32),
                        pltpu.VMEM((ni, CHUNK, CHUNK), F32),
                        pltpu.VMEM((ni, CHUNK, CHUNK), F32),
                        pltpu.VMEM((ni, 2 * CHUNK, CHUNK), BF16),
                        pltpu.VMEM((ni, CHUNK, CHUNK), BF16),
                        pltpu.VMEM((ni, CHUNK, 2 * dk), F32),
                        pltpu.VMEM((ni, CHUNK, 2 * dk), BF16),
                        pltpu.VMEM((ni, CHUNK, dk), F32),
                        pltpu.VMEM((ni, 2 * CHUNK, dk), BF16),
                        pltpu.VMEM((ni, dk, CHUNK), BF16),
                        pltpu.VMEM((ni, 1, dk), F32),
                        pltpu.VMEM((B_HEADS, CHUNK, dk), BF16),
                        pltpu.VMEM((B_HEADS, CHUNK, dk), F32)],
        compiler_params=_cparams("parallel", "arbitrary"),
        name="deltanet",
    )(proj, proj, proj, proj, p2, conv_w.astype(F32), alog, dtb, norm_g.reshape(1, -1).astype(F32))


def _t5_bucket(dist):
    max_exact = REL_BUCKETS // 2
    n = np.maximum(dist, 0)
    large = max_exact + (np.log(np.maximum(n, 1) / max_exact) / math.log(REL_MAX_DIST / max_exact)
                         * (REL_BUCKETS - max_exact)).astype(np.int32)
    large = np.minimum(large, REL_BUCKETS - 1)
    return np.where(n < max_exact, n, large).astype(np.int32)


def _band_pattern(window, dil):
    i = np.arange(C_BLOCK)[:, None]
    kk = np.arange(2 * C_BLOCK)[None, :]
    j = C_BLOCK + i - kk
    band = (j >= 0) & (j <= window // dil)
    bucket = _t5_bucket(np.clip(j, 0, None) * dil)
    return band, bucket


def _attn_kernel(q0_ref, q1_ref, q2_ref, k0_ref, k1_ref, k2_ref, kp0_ref, kp1_ref, kp2_ref, v_ref, vp_ref,
                 bias_ref, o_ref, s_ref, p_ref, inv_ref, og_ref, lse_ref):
    tt = q0_ref.shape[0]
    first = pl.program_id(1) == 0
    q_refs = (q0_ref, q1_ref, q2_ref)
    k_refs = (k0_ref, k1_ref, k2_ref)
    kp_refs = (kp0_ref, kp1_ref, kp2_ref)
    scale = C_HEAD_DIM ** -0.5
    shape = (C_BLOCK, C_HEAD_DIM)
    lo, hi = slice(0, C_BLOCK), slice(C_BLOCK, 2 * C_BLOCK)

    for g, (_, dil) in enumerate(C_PATTERNS):
        q_ref, k_ref, kp_ref = q_refs[g], k_refs[g], kp_refs[g]
        span = C_BLOCK * dil
        nblk = tt // span
        def rows(start, dil=dil):
            return pl.ds(start, C_BLOCK, stride=dil) if dil > 1 else pl.ds(start, C_BLOCK)

        blocks = [(rows(r + n * span), rows(r + (n - 1) * span) if n else rows(r + (nblk - 1) * span), n == 0)
                  for r in range(dil) for n in range(nblk)]

        for j, (sl, psl, from_prev_tile) in enumerate(blocks):
            q = (q_ref[sl, :] * scale).astype(BF16)
            kp = (kp_ref if from_prev_tile else k_ref)[psl, :].astype(BF16)
            s_p = _dot_nt(q, kp) + bias_ref[g, 0, :, lo]
            if from_prev_tile:
                s_p = jnp.where(first, -jnp.inf, s_p)
            s_ref[j, :, lo] = s_p
            s_ref[j, :, hi] = _dot_nt(q, k_ref[sl, :].astype(BF16)) + bias_ref[g, 0, :, hi]

        for j, (sl, psl, from_prev_tile) in enumerate(blocks):
            s = s_ref[j]
            mx = jnp.max(s, axis=-1, keepdims=True)
            p = jnp.exp(s - mx)
            den = jnp.sum(p, axis=-1, keepdims=True)
            p_ref[j] = p.astype(BF16)
            inv_ref[j] = jnp.broadcast_to(1.0 / den, shape)
            lse_ref[g, sl, :] = jnp.broadcast_to(mx + jnp.log(den), shape)

        for j, (sl, psl, from_prev_tile) in enumerate(blocks):
            vp = (vp_ref if from_prev_tile else v_ref)[psl, :].astype(BF16)
            num = _dot(p_ref[j, :, lo], vp) + _dot(p_ref[j, :, hi], v_ref[sl, :].astype(BF16))
            og_ref[g, sl, :] = num * inv_ref[j]

    piece = 2 * C_BLOCK
    for i in range(tt // piece):
        sl = slice(i * piece, (i + 1) * piece)
        l0, l1, l2 = lse_ref[0, sl, :], lse_ref[1, sl, :], lse_ref[2, sl, :]
        lm = jnp.maximum(jnp.maximum(l0, l1), l2)
        e0, e1, e2 = jnp.exp(l0 - lm), jnp.exp(l1 - lm), jnp.exp(l2 - lm)
        num = e0 * og_ref[0, sl, :] + e1 * og_ref[1, sl, :] + e2 * og_ref[2, sl, :]
        o_ref[sl, :] = (num / (e0 + e1 + e2)).astype(o_ref.dtype)


def _dilated_attn(cproj, rel_bias, batch, seq):
    tt = TT_ATTN
    assert seq % tt == 0 and tt == C_BLOCK * max(d for _, d in C_PATTERNS)
    nt = seq // tt
    biases = []
    for gi, (window, dil) in enumerate(C_PATTERNS):
        band, bucket = _band_pattern(window, dil)
        onehot = jnp.asarray(bucket[..., None] == np.arange(REL_BUCKETS), F32)
        table = rel_bias[:, gi * C_HEADS:(gi + 1) * C_HEADS].astype(F32)
        bias = jnp.einsum('qkn,nh->hqk', onehot, table, precision=lax.Precision.HIGHEST)
        biases.append(jnp.where(band[None], bias, -jnp.inf))
    bias = jnp.stack(biases)

    def cur(off, g=0):
        c0 = off // C_HEAD_DIM + g * C_HEADS
        return pl.BlockSpec((tt, C_HEAD_DIM), lambda b, t, h: (b * nt + t, c0 + h))

    def prev(off, g=0):
        c0 = off // C_HEAD_DIM + g * C_HEADS
        return pl.BlockSpec((tt, C_HEAD_DIM), lambda b, t, h: (b * nt + jnp.maximum(t - 1, 0), c0 + h))

    ng = len(C_PATTERNS)
    nblocks = tt // C_BLOCK
    in_specs = ([cur(CQ_OFF, g) for g in range(ng)] + [cur(CK_OFF, g) for g in range(ng)]
                + [prev(CK_OFF, g) for g in range(ng)] + [cur(CV_OFF), prev(CV_OFF)]
                + [pl.BlockSpec((ng, 1, C_BLOCK, 2 * C_BLOCK), lambda b, t, h: (0, h, 0, 0))])
    return pl.pallas_call(
        _attn_kernel,
        grid=(batch, nt, C_HEADS),
        in_specs=in_specs,
        out_specs=pl.BlockSpec((tt, C_HEAD_DIM), lambda b, t, h: (b * nt + t, h)),
        out_shape=jax.ShapeDtypeStruct((batch * seq, C_WIDTH), BF16),
        scratch_shapes=[pltpu.VMEM((nblocks, C_BLOCK, 2 * C_BLOCK), F32),
                        pltpu.VMEM((nblocks, C_BLOCK, 2 * C_BLOCK), BF16),
                        pltpu.VMEM((nblocks, C_BLOCK, C_HEAD_DIM), F32),
                        pltpu.VMEM((ng, tt, C_HEAD_DIM), F32),
                        pltpu.VMEM((ng, tt, C_HEAD_DIM), F32)],
        compiler_params=_cparams("parallel", "arbitrary", "arbitrary"),
        name="dilated_attn",
    )(*([cproj] * (3 * ng + 2)), bias)


def _merge_kernel(x_ref, g0_ref, g1_ref, g2_ref, au_ref, av_ref, ws_ref, bs_ref, lng_ref, lnb_ref, yb_ref, yc_ref,
                  wb_ref, wo_ref, out_ref, ya_ref):
    _gmlp_kernel(au_ref, av_ref, ws_ref, bs_ref, lng_ref, lnb_ref, ya_ref)
    merged = (_sigmoid(g0_ref[...].astype(F32)) * _dot(ya_ref[...], wb_ref[0])
              + _sigmoid(g1_ref[...].astype(F32)) * _dot(yb_ref[...], wb_ref[1])
              + _sigmoid(g2_ref[...].astype(F32)) * _dot(yc_ref[...], wb_ref[2]))
    out_ref[...] = x_ref[...] + _dot(merged.astype(BF16), wo_ref[...])


def _merge(x2d, proj, a_w_s, a_b_s, a_ln_g, a_ln_b, yb, yc, w_branch, w_out):
    n, d = x2d.shape
    tm = min(TM_MERGE, n)
    assert tm % A_CHUNK == 0
    row = lambda w: pl.BlockSpec((tm, w), lambda i: (i, 0))
    gate = lambda g: pl.BlockSpec((tm, d), lambda i: (i, GATE_OFF // d + g))
    full = lambda shape: pl.BlockSpec(shape, lambda i: (0,) * len(shape))
    causal = np.tril(np.ones((A_CHUNK, A_CHUNK), dtype=bool))
    w_s = jnp.where(causal, a_w_s, 0.0).astype(BF16)
    bias = jnp.repeat(jnp.transpose(a_b_s), LANES, axis=1).astype(F32)
    return pl.pallas_call(
        _merge_kernel,
        grid=(n // tm,),
        in_specs=[row(d), gate(0), gate(1), gate(2),
                  pl.BlockSpec((tm, WBLK), lambda i: (i, AU_OFF // WBLK)),
                  pl.BlockSpec((tm, WBLK), lambda i: (i, AV_OFF // WBLK)),
                  full((A_GROUPS, A_CHUNK, A_CHUNK)), full((A_CHUNK, A_WIDTH)), full((1, A_WIDTH)), full((1, A_WIDTH)),
                  row(WBLK), row(WBLK), full((3, WBLK, d)), full((d, d))],
        out_specs=row(d),
        out_shape=jax.ShapeDtypeStruct((n, d), F32),
        scratch_shapes=[pltpu.VMEM((tm, A_WIDTH), BF16)],
        compiler_params=_cparams("parallel"),
        name="merge",
    )(x2d, proj, proj, proj, proj, proj, w_s, bias, a_ln_g.reshape(1, -1), a_ln_b.reshape(1, -1), yb, yc,
      w_branch.astype(BF16), w_out.astype(BF16))


EXP_LANE0 = MOE_GROUPS
META_E, META_RANK, META_GATE = 0, 2, 4


ROUTER_ROWS = 40


def _router_kernel(x_ref, g_ref, whi_ref, wlo_ref, meta_ref, cnt_ref, run_ref):
    tm = x_ref.shape[0]
    nr = ROUTER_ROWS

    @pl.when(pl.program_id(0) == 0)
    def _():
        run_ref[...] = jnp.zeros_like(run_ref)

    x = x_ref[...]
    h = x * lax.rsqrt(jnp.mean(x * x, axis=-1, keepdims=True) + EPS) * g_ref[...]
    h_hi = h.astype(BF16)
    h_lo = (h - h_hi.astype(F32)).astype(BF16)
    w_hi = whi_ref[...]
    logits = (_dot_nt(w_hi, h_hi) + (_dot_nt(w_hi, h_lo) + _dot_nt(wlo_ref[...], h_hi)))[:nr]
    row_f = lax.broadcasted_iota(jnp.int32, (nr, tm), 0).astype(F32)
    big = float(LANES)

    def first_argmax(vals, mx):
        return jnp.min(jnp.where(vals == mx, row_f, big), axis=0, keepdims=True)

    gl = jnp.where(row_f < MOE_GROUPS, logits, -jnp.inf)
    gmax = jnp.max(gl, axis=0, keepdims=True)
    grp_p = 1.0 / jnp.sum(jnp.exp(gl - gmax), axis=0, keepdims=True)
    gidx = first_argmax(gl, gmax)
    lo = EXP_LANE0 + gidx * MOE_PER_GROUP
    in_grp = (row_f >= lo) & (row_f < lo + MOE_PER_GROUP)
    el = jnp.where(in_grp, logits, -jnp.inf)
    m1 = jnp.max(el, axis=0, keepdims=True)
    i1 = first_argmax(el, m1)
    el2 = jnp.where(row_f == i1, -jnp.inf, el)
    m2 = jnp.max(el2, axis=0, keepdims=True)
    i2 = first_argmax(el2, m2)
    t2 = jnp.exp(m2 - m1)
    gate1 = grp_p / (1.0 + t2)
    gate2 = grp_p * t2 / (1.0 + t2)
    oh1 = row_f == i1
    oh2 = row_f == i2
    onehot = jnp.where(oh1 | oh2, 1.0, 0.0)
    si = lax.broadcasted_iota(jnp.int32, (tm, tm), 0)
    ti = lax.broadcasted_iota(jnp.int32, (tm, tm), 1)
    before = jnp.where(si < ti, 1.0, 0.0).astype(BF16)
    prior = _dot(onehot.astype(BF16), before) + run_ref[:, 0:1]
    rank1 = jnp.sum(jnp.where(oh1, prior, 0.0), axis=0, keepdims=True)
    rank2 = jnp.sum(jnp.where(oh2, prior, 0.0), axis=0, keepdims=True)
    run_new = jnp.broadcast_to((prior + onehot)[:, tm - 1:tm], (nr, LANES))
    run_ref[...] = run_new
    cnt_ref[...] = run_new
    zrow = jnp.zeros((1, tm), F32)
    rec = jnp.concatenate([i1 - EXP_LANE0, i2 - EXP_LANE0, rank1, rank2, gate1, gate2, zrow, zrow]
                          + [jnp.zeros((LANES - SUBLANES, tm), F32)], axis=0)
    meta_ref[...] = rec.T


def _router(x2d, g, w_group, w_expert):
    n, d = x2d.shape
    tm = min(TM_ROUTER, n)
    w = jnp.transpose(jnp.concatenate([w_group, w_expert], axis=1).astype(F32))
    w = jnp.pad(w, ((0, LANES - w.shape[0]), (0, 0)))
    w_hi = w.astype(BF16)
    w_lo = (w - w_hi.astype(F32)).astype(BF16)
    return pl.pallas_call(
        _router_kernel,
        grid=(n // tm,),
        in_specs=[pl.BlockSpec((tm, d), lambda i: (i, 0)),
                  pl.BlockSpec((1, d), lambda i: (0, 0)),
                  pl.BlockSpec((LANES, d), lambda i: (0, 0)),
                  pl.BlockSpec((LANES, d), lambda i: (0, 0))],
        out_specs=[pl.BlockSpec((tm, LANES), lambda i: (i, 0)),
                   pl.BlockSpec((ROUTER_ROWS, LANES), lambda i: (0, 0))],
        out_shape=[jax.ShapeDtypeStruct((n, LANES), F32),
                   jax.ShapeDtypeStruct((ROUTER_ROWS, LANES), F32)],
        scratch_shapes=[pltpu.VMEM((ROUTER_ROWS, LANES), F32)],
        compiler_params=_cparams("arbitrary"),
        name="router",
    )(x2d, g.reshape(1, d), w_hi, w_lo)


DMA_UNROLL = 8


ROW_SUB = D_MODEL // LANES
assert ROW_SUB == SUBLANES


def _row_tile_store(ref, y):
    m = y.shape[0]
    for s in range(ROW_SUB):
        ref[pl.ds(s, m, stride=ROW_SUB), :] = y[:, s * LANES:(s + 1) * LANES]


def _row_tile_load(ref, m, s):
    return ref[pl.ds(s, m, stride=ROW_SUB), :]


def _row_of(ref, i):
    return ref.at[pl.ds(pl.multiple_of(i * ROW_SUB, ROW_SUB), ROW_SUB)]


def _start_rows(row_copy, m):
    def body(j, c):
        row_copy(j, 0).start(priority=0)
        row_copy(j, 1).start(priority=1)
        return c
    lax.fori_loop(0, m, body, 0, unroll=DMA_UNROLL)


def _wait_rows(row_copy, m):
    def body(j, c):
        row_copy(j, 0).wait()
        row_copy(j, 1).wait()
        return c
    lax.fori_loop(0, m, body, 0, unroll=DMA_UNROLL)


def _dispatch_kernel(experts_per_step, dest_ref, pad_lo_ref, pad_hi_ref, x_ref, g_ref, xs_hbm,
                     xta_ref, xtb_ref, zero_ref, sem_a, sem_b, sem_z):
    th = x_ref.shape[0] // 2
    i = pl.program_id(0)

    @pl.when(i == 0)
    def _():
        zero_ref[...] = jnp.zeros_like(zero_ref)

    for k in range(experts_per_step):
        e = i * experts_per_step + k

        @pl.when(e < MOE_EXPERTS)
        def _(e=e):
            lo, count = pad_lo_ref[e], pad_hi_ref[e] - pad_lo_ref[e]
            sizes = [BM_EXPERT >> (b + 1) for b in range(BM_EXPERT.bit_length() - 1)]
            copies, off = [], lo
            for size in sizes:
                copies.append(pltpu.make_async_copy(zero_ref.at[pl.ds(0, size)], xs_hbm.at[pl.ds(off, size)], sem_z))
                off = off + (count & size)
            for size, cp in zip(sizes, copies):
                @pl.when((count & size) != 0)
                def _(cp=cp):
                    cp.start()
            for size, cp in zip(sizes, copies):
                @pl.when((count & size) != 0)
                def _(cp=cp):
                    cp.wait()

    def normed(x):
        return x * lax.rsqrt(jnp.mean(x * x, axis=-1, keepdims=True) + EPS) * g_ref[...]

    def copy_a(j, k):
        return pltpu.make_async_copy(_row_of(xta_ref, j), xs_hbm.at[dest_ref[2 * j + k]], sem_a)

    def copy_b(j, k):
        return pltpu.make_async_copy(_row_of(xtb_ref, j), xs_hbm.at[dest_ref[2 * th + 2 * j + k]], sem_b)

    @pl.when(i > 0)
    def _():
        _wait_rows(copy_a, th)
    _row_tile_store(xta_ref, normed(x_ref[:th, :]))
    _start_rows(copy_a, th)

    @pl.when(i > 0)
    def _():
        _wait_rows(copy_b, th)
    _row_tile_store(xtb_ref, normed(x_ref[th:, :]))
    _start_rows(copy_b, th)

    @pl.when(i == pl.num_programs(0) - 1)
    def _():
        _wait_rows(copy_a, th)
        _wait_rows(copy_b, th)
        zrows = zero_ref.shape[0]
        used = pad_hi_ref[MOE_EXPERTS - 1]
        tail = [pltpu.make_async_copy(zero_ref, xs_hbm.at[pl.ds(used + t * zrows, zrows)], sem_z)
                for t in range(MOE_EXPERTS * BM_EXPERT // zrows)]
        total = xs_hbm.shape[0]
        for t, cp in enumerate(tail):
            @pl.when(used + t * zrows < total)
            def _(cp=cp):
                cp.start()
        for t, cp in enumerate(tail):
            @pl.when(used + t * zrows < total)
            def _(cp=cp):
                cp.wait()


def _dispatch(x2d, g, dest, pad_lo, pad_hi, rows):
    n, d = x2d.shape
    td = min(TD_DISPATCH, n)
    nsteps = n // td
    experts_per_step = -(-MOE_EXPERTS // nsteps)
    half_buf = pltpu.VMEM((td // 2 * ROW_SUB, LANES), F32)
    return pl.pallas_call(
        functools.partial(_dispatch_kernel, experts_per_step),
        grid=(nsteps,),
        in_specs=[pl.BlockSpec((2 * td,), lambda i: (i,), memory_space=pltpu.SMEM),
                  pl.BlockSpec(memory_space=pltpu.SMEM),
                  pl.BlockSpec(memory_space=pltpu.SMEM),
                  pl.BlockSpec((td, d), lambda i: (i, 0)),
                  pl.BlockSpec((1, d), lambda i: (0, 0))],
        out_specs=pl.BlockSpec(memory_space=pl.ANY),
        out_shape=jax.ShapeDtypeStruct((rows, ROW_SUB, LANES), F32),
        scratch_shapes=[half_buf, half_buf, pltpu.VMEM((BM_EXPERT // 2, ROW_SUB, LANES), F32),
                        pltpu.SemaphoreType.DMA, pltpu.SemaphoreType.DMA, pltpu.SemaphoreType.DMA],
        compiler_params=_cparams("arbitrary"),
        name="moe_dispatch",
    )(dest, pad_lo, pad_hi, x2d, g.reshape(1, d))


def _experts_kernel(be_ref, nused_ref, xs_ref, w1_ref, w3_ref, w2_ref, ys_ref, w1b_ref, w3b_ref, w2b_ref):
    i = pl.program_id(0)
    bm = xs_ref.shape[0] // ROW_SUB

    @pl.when(jnp.logical_or(i == 0, be_ref[i] != be_ref[jnp.maximum(i - 1, 0)]))
    def _():
        w1b_ref[...] = w1_ref[0].astype(BF16)
        w3b_ref[...] = w3_ref[0].astype(BF16)
        w2b_ref[...] = w2_ref[0].astype(BF16)

    @pl.when(i < nused_ref[0])
    def _():
        h = jnp.concatenate([_row_tile_load(xs_ref, bm, s) for s in range(ROW_SUB)], axis=1).astype(BF16)
        hid = _silu(_dot(h, w1b_ref[...])) * _dot(h, w3b_ref[...])
        _row_tile_store(ys_ref, _dot(hid.astype(BF16), w2b_ref[...]))

    @pl.when(i >= nused_ref[0])
    def _():
        ys_ref[...] = jnp.zeros_like(ys_ref)


def _experts(xs, blk_expert, n_used, w1, w3, w2):
    rows = xs.shape[0] // ROW_SUB
    d = D_MODEL
    nb = rows // BM_EXPERT
    ff = w1.shape[-1]
    blk = BM_EXPERT * ROW_SUB
    grid_spec = pltpu.PrefetchScalarGridSpec(
        num_scalar_prefetch=2,
        grid=(nb,),
        in_specs=[pl.BlockSpec((blk, LANES), lambda i, be, nu: (jnp.minimum(i, nu[0] - 1), 0)),
                  pl.BlockSpec((1, d, ff), lambda i, be, nu: (be[i], 0, 0)),
                  pl.BlockSpec((1, d, ff), lambda i, be, nu: (be[i], 0, 0)),
                  pl.BlockSpec((1, ff, d), lambda i, be, nu: (be[i], 0, 0))],
        out_specs=pl.BlockSpec((blk, LANES), lambda i, be, nu: (i, 0)),
        scratch_shapes=[pltpu.VMEM((d, ff), BF16), pltpu.VMEM((d, ff), BF16), pltpu.VMEM((ff, d), BF16)],
    )
    return pl.pallas_call(
        _experts_kernel,
        grid_spec=grid_spec,
        out_shape=jax.ShapeDtypeStruct((rows * ROW_SUB, LANES), F32),
        compiler_params=_cparams("arbitrary"),
        name="moe_experts",
    )(blk_expert, n_used, xs, w1, w3, w2)


def _combine_kernel(final, dest_ref, dest_next_ref, x_ref, meta_ref, ys_hbm, g_ref, o_ref,
                    ya0_ref, ya1_ref, yb0_ref, yb1_ref, sem_a, sem_b):
    th = x_ref.shape[0] // 2
    i = pl.program_id(0)

    def gather(dref, off, bufs, sem):
        def row_copy(j, k):
            return pltpu.make_async_copy(ys_hbm.at[dref[off + 2 * j + k]], _row_of(bufs[k], j), sem)
        return row_copy

    copy_a = gather(dest_ref, 0, (ya0_ref, ya1_ref), sem_a)
    copy_b = gather(dest_ref, 2 * th, (yb0_ref, yb1_ref), sem_b)
    copy_a_next = gather(dest_next_ref, 0, (ya0_ref, ya1_ref), sem_a)

    def combine(half, y0_ref, y1_ref):
        rows = slice(half * th, (half + 1) * th)
        meta = meta_ref[rows, :]
        gate0 = meta[:, META_GATE:META_GATE + 1]
        gate1 = meta[:, META_GATE + 1:META_GATE + 2]
        cols = [slice(s * LANES, (s + 1) * LANES) for s in range(ROW_SUB)]
        ys = [x_ref[rows, cols[s]] + gate0 * _row_tile_load(y0_ref, th, s) + gate1 * _row_tile_load(y1_ref, th, s)
              for s in range(ROW_SUB)]
        if final:
            ssq = sum(jnp.sum(y * y, axis=-1, keepdims=True) for y in ys)
            inv = lax.rsqrt(ssq * (1.0 / D_MODEL) + EPS)
            ys = [y * inv * g_ref[:, cols[s]] for s, y in enumerate(ys)]
        for s, y in enumerate(ys):
            o_ref[rows, cols[s]] = y

    @pl.when(i == 0)
    def _():
        _start_rows(copy_a, th)
    _start_rows(copy_b, th)
    _wait_rows(copy_a, th)
    combine(0, ya0_ref, ya1_ref)

    @pl.when(i + 1 < pl.num_programs(0))
    def _():
        _start_rows(copy_a_next, th)
    _wait_rows(copy_b, th)
    combine(1, yb0_ref, yb1_ref)


def _combine(x2d, meta, dest, ys, final_g, final):
    n, d = x2d.shape
    tc = min(TC_COMBINE, n)
    nsteps = n // tc
    half_buf = pltpu.VMEM((tc // 2 * ROW_SUB, LANES), F32)
    return pl.pallas_call(
        functools.partial(_combine_kernel, final),
        grid=(nsteps,),
        in_specs=[pl.BlockSpec((2 * tc,), lambda i: (i,), memory_space=pltpu.SMEM),
                  pl.BlockSpec((2 * tc,), lambda i: (jnp.minimum(i + 1, nsteps - 1),), memory_space=pltpu.SMEM),
                  pl.BlockSpec((tc, d), lambda i: (i, 0)),
                  pl.BlockSpec((tc, LANES), lambda i: (i, 0)),
                  pl.BlockSpec(memory_space=pl.ANY),
                  pl.BlockSpec((1, d), lambda i: (0, 0))],
        out_specs=pl.BlockSpec((tc, d), lambda i: (i, 0)),
        out_shape=jax.ShapeDtypeStruct((n, d), F32),
        scratch_shapes=[half_buf, half_buf, half_buf, half_buf,
                        pltpu.SemaphoreType.DMA, pltpu.SemaphoreType.DMA],
        compiler_params=_cparams("arbitrary"),
        name="moe_combine_final" if final else "moe_combine",
    )(dest, dest, x2d, meta, ys, final_g.reshape(1, d))


def _moe(x2d, norm_g, w_group, w_expert, w1, w3, w2, layer, final_g, final):
    n, d = x2d.shape
    meta, cnt = _router(x2d, norm_g, w_group, w_expert)
    counts = cnt[EXP_LANE0:EXP_LANE0 + MOE_EXPERTS, 0].astype(jnp.int32)
    padded = (counts + BM_EXPERT - 1) // BM_EXPERT * BM_EXPERT
    pad_end = jnp.cumsum(padded)
    pad_start = pad_end - padded
    e = meta[:, META_E:META_E + 2].astype(jnp.int32)
    rank = meta[:, META_RANK:META_RANK + 2].astype(jnp.int32)
    eids = jnp.arange(MOE_EXPERTS, dtype=jnp.int32)
    dest = (jnp.sum(jnp.where(e[..., None] == eids, pad_start, 0), axis=-1) + rank).reshape(2 * n)
    rows = 2 * n + MOE_EXPERTS * BM_EXPERT
    nb = rows // BM_EXPERT
    blk_start = jnp.arange(nb, dtype=jnp.int32) * BM_EXPERT
    blk_expert = jnp.minimum(jnp.sum((pad_end[None, :] <= blk_start[:, None]).astype(jnp.int32), axis=1),
                             MOE_EXPERTS - 1)
    n_used = (pad_end[-1:] // BM_EXPERT).astype(jnp.int32)
    xs = _dispatch(x2d, norm_g, dest, pad_start + counts, pad_end, rows)
    ys = _experts(xs.reshape(rows * ROW_SUB, LANES), blk_expert + layer * MOE_EXPERTS, n_used, w1, w3, w2)
    return _combine(x2d, meta, dest, ys.reshape(rows, ROW_SUB, LANES), final_g, final)


def _split_w_in(w):
    n_ab = 2 * A_WIDTH + 4 * B_WIDTH
    n_c = n_ab + 2 * B_HEADS
    n_g = n_c + 2 * 3 * C_WIDTH + C_WIDTH
    w_main = jnp.concatenate([w[:, n_g:], w[:, :n_ab]], axis=1).astype(BF16)
    w_c = w[:, n_c:n_g].astype(BF16)
    w_small = jnp.pad(w[:, n_ab:n_c], ((0, 0), (0, LANES - 2 * B_HEADS))).astype(BF16)
    return w_main, w_c, w_small


def kernel(x, norm_mix_g, w_in, a_ln_g, a_ln_b, a_w_s, a_b_s, b_conv_w, b_a_log, b_dt_bias, b_norm_g, rel_bias,
           w_branch, w_out, norm_ffn_g, w_group, w_expert, w1, w3, w2, final_norm_g):
    batch, seq, d = x.shape
    depth = w_in.shape[0]
    x2d = x.reshape(batch * seq, d)
    for l in range(depth):
        w_main, w_c, w_small = _split_w_in(w_in[l])
        assert w_main.shape[1] == PROJ_W and w_c.shape[1] == CPROJ_W
        proj, cproj, p2 = _in_proj(x2d, norm_mix_g[l], w_main, w_c, w_small)
        yb = _deltanet(proj, p2, b_conv_w[l], b_a_log[l], b_dt_bias[l], b_norm_g[l], batch, seq)
        yc = _dilated_attn(cproj, rel_bias, batch, seq)
        x2d = _merge(x2d, proj, a_w_s[l], a_b_s[l], a_ln_g[l], a_ln_b[l], yb, yc, w_branch[l], w_out[l])
        x2d = _moe(x2d, norm_ffn_g[l], w_group[l], w_expert[l], w1.reshape((-1,) + w1.shape[2:]),
                   w3.reshape((-1,) + w3.shape[2:]), w2.reshape((-1,) + w2.shape[2:]), l, final_norm_g,
                   final=(l == depth - 1))
    return x2d.reshape(batch, seq, d)
```

```python
import functools
import math

import numpy as np
import jax
import jax.numpy as jnp
from jax import lax
from jax.experimental import pallas as pl
from jax.experimental.pallas import tpu as pltpu

F32 = jnp.float32
BF16 = jnp.bfloat16

D_MODEL = 1024
A_GROUPS, A_CHUNK, A_WIDTH = 4, 128, 512
B_HEADS, B_HEAD_DIM, B_CONV, B_WIDTH = 4, 128, 4, 512
C_PATTERNS = ((128, 1), (512, 4), (2048, 16))
C_HEADS, C_HEAD_DIM, C_BLOCK, C_WIDTH = 4, 128, 128, 512
REL_BUCKETS, REL_MAX_DIST = 32, 2048
MOE_GROUPS, MOE_PER_GROUP, MOE_EXPERTS, MOE_FF = 4, 8, 32, 512
EPS = 1e-6

LANES = 128
SUBLANES = 8
VMEM_LIMIT = 48 * 1024 * 1024

PROJ_W = 6144
GATE_OFF, AU_OFF, AV_OFF = 0, 3072, 3584
BQ_OFF, BK_OFF, BV_OFF, BZ_OFF = 4096, 4608, 5120, 5632
WBLK = 512
CPROJ_W = 3584
CQ_OFF, CK_OFF, CV_OFF = 0, 1536, 3072

TM_PROJ, TN_PROJ = 2048, 512
TB_DELTA = 512
CHUNK = 128
TT_ATTN = 2048
TM_MERGE = 512
TM_ROUTER = 1024
TD_DISPATCH = 512
BM_EXPERT = 512
TC_COMBINE = 512


def _cparams(*sem):
    return pltpu.CompilerParams(dimension_semantics=sem, vmem_limit_bytes=VMEM_LIMIT)


def _dot(a, b):
    return jnp.dot(a, b, preferred_element_type=F32)


def _dot_nt(a, b):
    return lax.dot_general(a, b, (((1,), (1,)), ((), ())), preferred_element_type=F32)


def _sigmoid(x):
    return 1.0 / (1.0 + jnp.exp(-x))


def _silu(x):
    return x * _sigmoid(x)


def _gelu_tanh(x):
    c = math.sqrt(2.0 / math.pi)
    return x * (0.5 * (1.0 + jnp.tanh(c * (x + 0.044715 * (x * x * x)))))


def _softplus(x):
    return jnp.maximum(x, 0.0) + jnp.log(1.0 + jnp.exp(-jnp.abs(x)))


def _in_proj_kernel(nm, nc, x_ref, g_ref, wm_ref, wc_ref, ws_ref, proj_ref, cproj_ref, p2_ref, h_ref):
    j = pl.program_id(1)

    @pl.when(j == 0)
    def _():
        x = x_ref[...]
        ms = jnp.mean(x * x, axis=-1, keepdims=True)
        h_ref[...] = (x * lax.rsqrt(ms + EPS) * g_ref[...]).astype(h_ref.dtype)

    @pl.when(j < nm)
    def _():
        proj_ref[...] = _dot(h_ref[...], wm_ref[...]).astype(proj_ref.dtype)

    @pl.when(jnp.logical_and(j >= nm, j < nm + nc))
    def _():
        cproj_ref[...] = _dot(h_ref[...], wc_ref[...])

    @pl.when(j == nm + nc)
    def _():
        p2_ref[...] = _dot(h_ref[...], ws_ref[...])


def _in_proj(x2d, g, w_main, w_c, w_small):
    n, d = x2d.shape
    tm = min(TM_PROJ, n)
    tn = tn_main = TN_PROJ
    nm, nc = w_main.shape[1] // tn_main, w_c.shape[1] // tn
    main_col = lambda i, j: jnp.minimum(j, nm - 1)
    c_col = lambda i, j: jnp.clip(j - nm, 0, nc - 1)
    return pl.pallas_call(
        functools.partial(_in_proj_kernel, nm, nc),
        grid=(n // tm, nm + nc + 1),
        in_specs=[pl.BlockSpec((tm, d), lambda i, j: (i, 0)),
                  pl.BlockSpec((1, d), lambda i, j: (0, 0)),
                  pl.BlockSpec((d, tn_main), lambda i, j: (0, main_col(i, j))),
                  pl.BlockSpec((d, tn), lambda i, j: (0, c_col(i, j))),
                  pl.BlockSpec((d, LANES), lambda i, j: (0, 0))],
        out_specs=[pl.BlockSpec((tm, tn_main), lambda i, j: (i, main_col(i, j))),
                   pl.BlockSpec((tm, tn), lambda i, j: (i, c_col(i, j))),
                   pl.BlockSpec((tm, LANES), lambda i, j: (i, 0))],
        out_shape=[jax.ShapeDtypeStruct((n, w_main.shape[1]), BF16),
                   jax.ShapeDtypeStruct((n, w_c.shape[1]), F32),
                   jax.ShapeDtypeStruct((n, LANES), F32)],
        scratch_shapes=[pltpu.VMEM((tm, d), BF16)],
        compiler_params=_cparams("parallel", "arbitrary"),
        name="in_proj",
    )(x2d, g.reshape(1, d), w_main, w_c, w_small)


def _gmlp_kernel(u_ref, v_ref, w_ref, b_ref, lng_ref, lnb_ref, o_ref):
    tm = u_ref.shape[0]
    u = _gelu_tanh(u_ref[...].astype(F32))
    v = _gelu_tanh(v_ref[...].astype(F32))
    mu = jnp.mean(v, axis=-1, keepdims=True)
    vc = v - mu
    var = jnp.mean(vc * vc, axis=-1, keepdims=True)
    vn = (vc * lax.rsqrt(var + EPS) * lng_ref[...] + lnb_ref[...]).astype(BF16)
    for c in range(tm // A_CHUNK):
        rows = slice(c * A_CHUNK, (c + 1) * A_CHUNK)
        for g in range(A_GROUPS):
            cols = slice(g * LANES, (g + 1) * LANES)
            sv = _dot(w_ref[g], vn[rows, cols]) + b_ref[:, cols]
            o_ref[rows, cols] = (u[rows, cols] * sv).astype(o_ref.dtype)


def _deltanet_kernel(q_ref, k_ref, v_ref, z_ref, p2_ref, cw_ref, alog_ref, dtb_ref, ng_ref, o_ref,
                     s_ref, carry_ref, qkv_ref, gc_ref, gct_ref, beta_ref, r_ref, p_ref, stack_ref, attn_ref,
                     rhs_ref, rhsb_ref, u_ref, wq_ref, kt_ref, eg_ref, vn_ref, op_ref):
    tb = q_ref.shape[0]
    nchunk = tb // CHUNK

    @pl.when(pl.program_id(1) == 0)
    def _():
        s_ref[...] = jnp.zeros_like(s_ref)
        carry_ref[...] = jnp.zeros_like(carry_ref)

    row8 = lax.broadcasted_iota(jnp.int32, (SUBLANES, B_WIDTH), 0)
    for idx, ref in enumerate((q_ref, k_ref, v_ref)):
        x = ref[...].astype(F32)
        w = cw_ref[:, idx * B_WIDTH:(idx + 1) * B_WIDTH]
        prev = carry_ref[idx]
        acc = x * w[B_CONV - 1:B_CONV]
        for s in range(1, B_CONV):
            xs = pltpu.roll(x, s, axis=0)
            ps = pltpu.roll(prev, s, axis=0)
            head = jnp.where(row8 < s, ps, xs[:SUBLANES])
            xs = jnp.concatenate([head, xs[SUBLANES:]], axis=0)
            acc = acc + xs * w[B_CONV - 1 - s:B_CONV - s]
        carry_ref[idx] = x[tb - SUBLANES:]
        qkv_ref[idx] = _silu(acc)

    ri = lax.broadcasted_iota(jnp.int32, (CHUNK, CHUNK), 0)
    ci = lax.broadcasted_iota(jnp.int32, (CHUNK, CHUNK), 1)
    incl = ri >= ci
    strict = ri > ci
    tril_f = incl.astype(F32)
    alog = alog_ref[...]
    dtb = dtb_ref[...]
    ng = ng_ref[...]
    lo, hi = slice(0, CHUNK), slice(CHUNK, 2 * CHUNK)
    items = [(c, h) for c in range(nchunk) for h in range(B_HEADS)]

    for c in range(nchunk):
        rows = slice(c * CHUNK, (c + 1) * CHUNK)
        p2 = p2_ref[rows, :]
        g_all = -jnp.exp(alog) * _softplus(p2 + dtb)
        gc_all = jnp.dot(tril_f, g_all, preferred_element_type=F32, precision=lax.Precision.HIGHEST)
        gc_ref[c] = gc_all
        gct_ref[c] = gc_all.T
        beta_ref[c] = _sigmoid(p2)

    for i, (c, h) in enumerate(items):
        rows = slice(c * CHUNK, (c + 1) * CHUNK)
        cols = slice(h * B_HEAD_DIM, (h + 1) * B_HEAD_DIM)
        q = qkv_ref[0, rows, cols]
        k = qkv_ref[1, rows, cols]
        v = qkv_ref[2, rows, cols]
        q = q * (lax.rsqrt(jnp.sum(q * q, axis=-1, keepdims=True) + EPS) * (B_HEAD_DIM ** -0.5))
        k = k * lax.rsqrt(jnp.sum(k * k, axis=-1, keepdims=True) + EPS)
        beta = beta_ref[c, :, h:h + 1]
        gc_col = jnp.broadcast_to(gc_ref[c, :, B_HEADS + h:B_HEADS + h + 1], (CHUNK, B_HEAD_DIM))
        gc_row = gct_ref[c, B_HEADS + h:B_HEADS + h + 1, :]
        decay = jnp.exp(jnp.where(incl, gc_col - gc_row, -jnp.inf))
        kb = k * beta
        vb = v * beta
        egc = jnp.exp(gc_col)
        aq = _dot_nt(jnp.concatenate([kb, q], axis=0).astype(BF16), k.astype(BF16))
        m = -jnp.where(strict, aq[lo] * decay, 0.0)
        r_ref[i] = m
        stack_ref[i, lo] = m.astype(BF16)
        attn_ref[i] = (aq[hi] * decay).astype(BF16)
        rhs = jnp.concatenate([vb, kb * egc], axis=1)
        rhs_ref[i] = rhs
        rhsb_ref[i] = rhs.astype(BF16)
        wq_ref[i, hi] = (q * egc).astype(BF16)
        g_last = gc_col[CHUNK - 1:CHUNK, :]
        kt_ref[i] = (k * jnp.exp(g_last - gc_col)).T.astype(BF16)
        eg_ref[i] = jnp.exp(g_last)

    for i in range(len(items)):
        mb = stack_ref[i, lo]
        p = _dot(mb, mb)
        p_ref[i] = p
        stack_ref[i, hi] = p.astype(BF16)
    for _ in range(5):
        for i in range(len(items)):
            rp = _dot(stack_ref[i], stack_ref[i, hi])
            r = r_ref[i] + p_ref[i] + rp[lo]
            p = rp[hi]
            r_ref[i] = r
            p_ref[i] = p
            stack_ref[i, lo] = r.astype(BF16)
            stack_ref[i, hi] = p.astype(BF16)
    for i in range(len(items)):
        r = r_ref[i] + p_ref[i] + _dot(stack_ref[i, lo], stack_ref[i, hi])
        sol = rhs_ref[i] + _dot(r.astype(BF16), rhsb_ref[i])
        u_ref[i] = sol[:, :B_HEAD_DIM]
        wq_ref[i, lo] = sol[:, B_HEAD_DIM:].astype(BF16)

    for c in range(nchunk):
        rows = slice(c * CHUNK, (c + 1) * CHUNK)
        for h in range(B_HEADS):
            i = c * B_HEADS + h
            wq = _dot(wq_ref[i], s_ref[h].astype(BF16))
            vn_ref[h] = (u_ref[i] - wq[lo]).astype(BF16)
            op_ref[h] = wq[hi]
        for h in range(B_HEADS):
            i = c * B_HEADS + h
            cols = slice(h * B_HEAD_DIM, (h + 1) * B_HEAD_DIM)
            vnb = vn_ref[h]
            o = op_ref[h] + _dot(attn_ref[i], vnb)
            s_ref[h] = s_ref[h] * eg_ref[i] + _dot(kt_ref[i], vnb)
            z = z_ref[rows, cols].astype(F32)
            on = o * lax.rsqrt(jnp.mean(o * o, axis=-1, keepdims=True) + EPS) * ng
            o_ref[rows, cols] = (on * _silu(z)).astype(o_ref.dtype)


def _deltanet(proj, p2, conv_w, a_log, dt_bias, norm_g, batch, seq):
    tb = min(TB_DELTA, seq)
    nt = seq // tb
    nchunk = tb // CHUNK
    ni = nchunk * B_HEADS
    dk = B_HEAD_DIM

    def col(off):
        return pl.BlockSpec((tb, WBLK), lambda b, t: (b * nt + t, off // WBLK))

    pad = LANES - 2 * B_HEADS
    alog = jnp.pad(a_log.astype(F32), (B_HEADS, pad)).reshape(1, LANES)
    dtb = jnp.pad(dt_bias.astype(F32), (B_HEADS, pad)).reshape(1, LANES)
    return pl.pallas_call(
        _deltanet_kernel,
        grid=(batch, nt),
        in_specs=[col(BQ_OFF), col(BK_OFF), col(BV_OFF), col(BZ_OFF),
                  pl.BlockSpec((tb, LANES), lambda b, t: (b * nt + t, 0)),
                  pl.BlockSpec((B_CONV, 3 * B_WIDTH), lambda b, t: (0, 0)),
                  pl.BlockSpec((1, LANES), lambda b, t: (0, 0)),
                  pl.BlockSpec((1, LANES), lambda b, t: (0, 0)),
                  pl.BlockSpec((1, B_HEAD_DIM), lambda b, t: (0, 0))],
        out_specs=pl.BlockSpec((tb, B_WIDTH), lambda b, t: (b * nt + t, 0)),
        out_shape=jax.ShapeDtypeStruct((batch * seq, B_WIDTH), BF16),
        scratch_shapes=[pltpu.VMEM((B_HEADS, dk, dk), F32),
                        pltpu.VMEM((3, SUBLANES, B_WIDTH), F32),
                        pltpu.VMEM((3, tb, B_WIDTH), F32),
                        pltpu.VMEM((nchunk, CHUNK, LANES), F32),
                        pltpu.VMEM((nchunk, LANES, CHUNK), F32),
                        pltpu.VMEM((nchunk, CHUNK, LANES), F32),
                        pltpu.VMEM((ni, CHUNK, CHUNK), F32),
                        pltpu.VMEM((ni, CHUNK, CHUNK), F32),
                        pltpu.VMEM((ni, 2 * CHUNK, CHUNK), BF16),
                        pltpu.VMEM((ni, CHUNK, CHUNK), BF16),
                        pltpu.VMEM((ni, CHUNK, 2 * dk), F32),
                        pltpu.VMEM((ni, CHUNK, 2 * dk), BF16),
                        pltpu.VMEM((ni, CHUNK, dk), F32),
                        pltpu.VMEM((ni, 2 * CHUNK, dk), BF16),
                        pltpu.VMEM((ni, dk, CHUNK), BF16),
                        pltpu.VMEM((ni, 1, dk), F32),
                        pltpu.VMEM((B_HEADS, CHUNK, dk), BF16),
                        pltpu.VMEM((B_HEADS, CHUNK, dk), F32)],
        compiler_params=_cparams("parallel", "arbitrary"),
        name="deltanet",
    )(proj, proj, proj, proj, p2, conv_w.astype(F32), alog, dtb, norm_g.reshape(1, -1).astype(F32))


def _t5_bucket(dist):
    max_exact = REL_BUCKETS // 2
    n = np.maximum(dist, 0)
    large = max_exact + (np.log(np.maximum(n, 1) / max_exact) / math.log(REL_MAX_DIST / max_exact)
                         * (REL_BUCKETS - max_exact)).astype(np.int32)
    large = np.minimum(large, REL_BUCKETS - 1)
    return np.where(n < max_exact, n, large).astype(np.int32)


def _band_pattern(window, dil):
    i = np.arange(C_BLOCK)[:, None]
    kk = np.arange(2 * C_BLOCK)[None, :]
    j = C_BLOCK + i - kk
    band = (j >= 0) & (j <= window // dil)
    bucket = _t5_bucket(np.clip(j, 0, None) * dil)
    return band, bucket


def _attn_kernel(q0_ref, q1_ref, q2_ref, k0_ref, k1_ref, k2_ref, kp0_ref, kp1_ref, kp2_ref, v_ref, vp_ref,
                 bias_ref, o_ref, s_ref, p_ref, inv_ref, og_ref, lse_ref):
    tt = q0_ref.shape[0]
    first = pl.program_id(1) == 0
    q_refs = (q0_ref, q1_ref, q2_ref)
    k_refs = (k0_ref, k1_ref, k2_ref)
    kp_refs = (kp0_ref, kp1_ref, kp2_ref)
    scale = C_HEAD_DIM ** -0.5
    shape = (C_BLOCK, C_HEAD_DIM)
    lo, hi = slice(0, C_BLOCK), slice(C_BLOCK, 2 * C_BLOCK)

    for g, (_, dil) in enumerate(C_PATTERNS):
        q_ref, k_ref, kp_ref = q_refs[g], k_refs[g], kp_refs[g]
        span = C_BLOCK * dil
        nblk = tt // span
        def rows(start, dil=dil):
            return pl.ds(start, C_BLOCK, stride=dil) if dil > 1 else pl.ds(start, C_BLOCK)

        blocks = [(rows(r + n * span), rows(r + (n - 1) * span) if n else rows(r + (nblk - 1) * span), n == 0)
                  for r in range(dil) for n in range(nblk)]

        for j, (sl, psl, from_prev_tile) in enumerate(blocks):
            q = (q_ref[sl, :] * scale).astype(BF16)
            kp = (kp_ref if from_prev_tile else k_ref)[psl, :].astype(BF16)
            s_p = _dot_nt(q, kp) + bias_ref[g, 0, :, lo]
            if from_prev_tile:
                s_p = jnp.where(first, -jnp.inf, s_p)
            s_ref[j, :, lo] = s_p
            s_ref[j, :, hi] = _dot_nt(q, k_ref[sl, :].astype(BF16)) + bias_ref[g, 0, :, hi]

        for j, (sl, psl, from_prev_tile) in enumerate(blocks):
            s = s_ref[j]
            mx = jnp.max(s, axis=-1, keepdims=True)
            p = jnp.exp(s - mx)
            den = jnp.sum(p, axis=-1, keepdims=True)
            p_ref[j] = p.astype(BF16)
            inv_ref[j] = jnp.broadcast_to(1.0 / den, shape)
            lse_ref[g, sl, :] = jnp.broadcast_to(mx + jnp.log(den), shape)

        for j, (sl, psl, from_prev_tile) in enumerate(blocks):
            vp = (vp_ref if from_prev_tile else v_ref)[psl, :].astype(BF16)
            num = _dot(p_ref[j, :, lo], vp) + _dot(p_ref[j, :, hi], v_ref[sl, :].astype(BF16))
            og_ref[g, sl, :] = num * inv_ref[j]

    piece = 2 * C_BLOCK
    for i in range(tt // piece):
        sl = slice(i * piece, (i + 1) * piece)
        l0, l1, l2 = lse_ref[0, sl, :], lse_ref[1, sl, :], lse_ref[2, sl, :]
        lm = jnp.maximum(jnp.maximum(l0, l1), l2)
        e0, e1, e2 = jnp.exp(l0 - lm), jnp.exp(l1 - lm), jnp.exp(l2 - lm)
        num = e0 * og_ref[0, sl, :] + e1 * og_ref[1, sl, :] + e2 * og_ref[2, sl, :]
        o_ref[sl, :] = (num / (e0 + e1 + e2)).astype(o_ref.dtype)


def _dilated_attn(cproj, rel_bias, batch, seq):
    tt = TT_ATTN
    assert seq % tt == 0 and tt == C_BLOCK * max(d for _, d in C_PATTERNS)
    nt = seq // tt
    biases = []
    for gi, (window, dil) in enumerate(C_PATTERNS):
        band, bucket = _band_pattern(window, dil)
        onehot = jnp.asarray(bucket[..., None] == np.arange(REL_BUCKETS), F32)
        table = rel_bias[:, gi * C_HEADS:(gi + 1) * C_HEADS].astype(F32)
        bias = jnp.einsum('qkn,nh->hqk', onehot, table, precision=lax.Precision.HIGHEST)
        biases.append(jnp.where(band[None], bias, -jnp.inf))
    bias = jnp.stack(biases)

    def cur(off, g=0):
        c0 = off // C_HEAD_DIM + g * C_HEADS
        return pl.BlockSpec((tt, C_HEAD_DIM), lambda b, t, h: (b * nt + t, c0 + h))

    def prev(off, g=0):
        c0 = off // C_HEAD_DIM + g * C_HEADS
        return pl.BlockSpec((tt, C_HEAD_DIM), lambda b, t, h: (b * nt + jnp.maximum(t - 1, 0), c0 + h))

    ng = len(C_PATTERNS)
    nblocks = tt // C_BLOCK
    in_specs = ([cur(CQ_OFF, g) for g in range(ng)] + [cur(CK_OFF, g) for g in range(ng)]
                + [prev(CK_OFF, g) for g in range(ng)] + [cur(CV_OFF), prev(CV_OFF)]
                + [pl.BlockSpec((ng, 1, C_BLOCK, 2 * C_BLOCK), lambda b, t, h: (0, h, 0, 0))])
    return pl.pallas_call(
        _attn_kernel,
        grid=(batch, nt, C_HEADS),
        in_specs=in_specs,
        out_specs=pl.BlockSpec((tt, C_HEAD_DIM), lambda b, t, h: (b * nt + t, h)),
        out_shape=jax.ShapeDtypeStruct((batch * seq, C_WIDTH), BF16),
        scratch_shapes=[pltpu.VMEM((nblocks, C_BLOCK, 2 * C_BLOCK), F32),
                        pltpu.VMEM((nblocks, C_BLOCK, 2 * C_BLOCK), BF16),
                        pltpu.VMEM((nblocks, C_BLOCK, C_HEAD_DIM), F32),
                        pltpu.VMEM((ng, tt, C_HEAD_DIM), F32),
                        pltpu.VMEM((ng, tt, C_HEAD_DIM), F32)],
        compiler_params=_cparams("parallel", "arbitrary", "arbitrary"),
        name="dilated_attn",
    )(*([cproj] * (3 * ng + 2)), bias)


def _merge_kernel(x_ref, g0_ref, g1_ref, g2_ref, au_ref, av_ref, ws_ref, bs_ref, lng_ref, lnb_ref, yb_ref, yc_ref,
                  wb_ref, wo_ref, out_ref, ya_ref):
    _gmlp_kernel(au_ref, av_ref, ws_ref, bs_ref, lng_ref, lnb_ref, ya_ref)
    merged = (_sigmoid(g0_ref[...].astype(F32)) * _dot(ya_ref[...], wb_ref[0])
              + _sigmoid(g1_ref[...].astype(F32)) * _dot(yb_ref[...], wb_ref[1])
              + _sigmoid(g2_ref[...].astype(F32)) * _dot(yc_ref[...], wb_ref[2]))
    out_ref[...] = x_ref[...] + _dot(merged.astype(BF16), wo_ref[...])


def _merge(x2d, proj, a_w_s, a_b_s, a_ln_g, a_ln_b, yb, yc, w_branch, w_out):
    n, d = x2d.shape
    tm = min(TM_MERGE, n)
    assert tm % A_CHUNK == 0
    row = lambda w: pl.BlockSpec((tm, w), lambda i: (i, 0))
    gate = lambda g: pl.BlockSpec((tm, d), lambda i: (i, GATE_OFF // d + g))
    full = lambda shape: pl.BlockSpec(shape, lambda i: (0,) * len(shape))
    causal = np.tril(np.ones((A_CHUNK, A_CHUNK), dtype=bool))
    w_s = jnp.where(causal, a_w_s, 0.0).astype(BF16)
    bias = jnp.repeat(jnp.transpose(a_b_s), LANES, axis=1).astype(F32)
    return pl.pallas_call(
        _merge_kernel,
        grid=(n // tm,),
        in_specs=[row(d), gate(0), gate(1), gate(2),
                  pl.BlockSpec((tm, WBLK), lambda i: (i, AU_OFF // WBLK)),
                  pl.BlockSpec((tm, WBLK), lambda i: (i, AV_OFF // WBLK)),
                  full((A_GROUPS, A_CHUNK, A_CHUNK)), full((A_CHUNK, A_WIDTH)), full((1, A_WIDTH)), full((1, A_WIDTH)),
                  row(WBLK), row(WBLK), full((3, WBLK, d)), full((d, d))],
        out_specs=row(d),
        out_shape=jax.ShapeDtypeStruct((n, d), F32),
        scratch_shapes=[pltpu.VMEM((tm, A_WIDTH), BF16)],
        compiler_params=_cparams("parallel"),
        name="merge",
    )(x2d, proj, proj, proj, proj, proj, w_s, bias, a_ln_g.reshape(1, -1), a_ln_b.reshape(1, -1), yb, yc,
      w_branch.astype(BF16), w_out.astype(BF16))


EXP_LANE0 = MOE_GROUPS
META_E, META_RANK, META_GATE = 0, 2, 4


ROUTER_ROWS = 40


def _router_kernel(x_ref, g_ref, whi_ref, wlo_ref, meta_ref, cnt_ref, run_ref):
    tm = x_ref.shape[0]
    nr = ROUTER_ROWS

    @pl.when(pl.program_id(0) == 0)
    def _():
        run_ref[...] = jnp.zeros_like(run_ref)

    x = x_ref[...]
    h = x * lax.rsqrt(jnp.mean(x * x, axis=-1, keepdims=True) + EPS) * g_ref[...]
    h_hi = h.astype(BF16)
    h_lo = (h - h_hi.astype(F32)).astype(BF16)
    w_hi = whi_ref[...]
    logits = (_dot_nt(w_hi, h_hi) + (_dot_nt(w_hi, h_lo) + _dot_nt(wlo_ref[...], h_hi)))[:nr]
    row_f = lax.broadcasted_iota(jnp.int32, (nr, tm), 0).astype(F32)
    big = float(LANES)

    def first_argmax(vals, mx):
        return jnp.min(jnp.where(vals == mx, row_f, big), axis=0, keepdims=True)

    gl = jnp.where(row_f < MOE_GROUPS, logits, -jnp.inf)
    gmax = jnp.max(gl, axis=0, keepdims=True)
    grp_p = 1.0 / jnp.sum(jnp.exp(gl - gmax), axis=0, keepdims=True)
    gidx = first_argmax(gl, gmax)
    lo = EXP_LANE0 + gidx * MOE_PER_GROUP
    in_grp = (row_f >= lo) & (row_f < lo + MOE_PER_GROUP)
    el = jnp.where(in_grp, logits, -jnp.inf)
    m1 = jnp.max(el, axis=0, keepdims=True)
    i1 = first_argmax(el, m1)
    el2 = jnp.where(row_f == i1, -jnp.inf, el)
    m2 = jnp.max(el2, axis=0, keepdims=True)
    i2 = first_argmax(el2, m2)
    t2 = jnp.exp(m2 - m1)
    gate1 = grp_p / (1.0 + t2)
    gate2 = grp_p * t2 / (1.0 + t2)
    oh1 = row_f == i1
    oh2 = row_f == i2
    onehot = jnp.where(oh1 | oh2, 1.0, 0.0)
    si = lax.broadcasted_iota(jnp.int32, (tm, tm), 0)
    ti = lax.broadcasted_iota(jnp.int32, (tm, tm), 1)
    before = jnp.where(si < ti, 1.0, 0.0).astype(BF16)
    prior = _dot(onehot.astype(BF16), before) + run_ref[:, 0:1]
    rank1 = jnp.sum(jnp.where(oh1, prior, 0.0), axis=0, keepdims=True)
    rank2 = jnp.sum(jnp.where(oh2, prior, 0.0), axis=0, keepdims=True)
    run_new = jnp.broadcast_to((prior + onehot)[:, tm - 1:tm], (nr, LANES))
    run_ref[...] = run_new
    cnt_ref[...] = run_new
    zrow = jnp.zeros((1, tm), F32)
    rec = jnp.concatenate([i1 - EXP_LANE0, i2 - EXP_LANE0, rank1, rank2, gate1, gate2, zrow, zrow]
                          + [jnp.zeros((LANES - SUBLANES, tm), F32)], axis=0)
    meta_ref[...] = rec.T


def _router(x2d, g, w_group, w_expert):
    n, d = x2d.shape
    tm = min(TM_ROUTER, n)
    w = jnp.transpose(jnp.concatenate([w_group, w_expert], axis=1).astype(F32))
    w = jnp.pad(w, ((0, LANES - w.shape[0]), (0, 0)))
    w_hi = w.astype(BF16)
    w_lo = (w - w_hi.astype(F32)).astype(BF16)
    return pl.pallas_call(
        _router_kernel,
        grid=(n // tm,),
        in_specs=[pl.BlockSpec((tm, d), lambda i: (i, 0)),
                  pl.BlockSpec((1, d), lambda i: (0, 0)),
                  pl.BlockSpec((LANES, d), lambda i: (0, 0)),
                  pl.BlockSpec((LANES, d), lambda i: (0, 0))],
        out_specs=[pl.BlockSpec((tm, LANES), lambda i: (i, 0)),
                   pl.BlockSpec((ROUTER_ROWS, LANES), lambda i: (0, 0))],
        out_shape=[jax.ShapeDtypeStruct((n, LANES), F32),
                   jax.ShapeDtypeStruct((ROUTER_ROWS, LANES), F32)],
        scratch_shapes=[pltpu.VMEM((ROUTER_ROWS, LANES), F32)],
        compiler_params=_cparams("arbitrary"),
        name="router",
    )(x2d, g.reshape(1, d), w_hi, w_lo)


DMA_UNROLL = 8


ROW_SUB = D_MODEL // LANES
assert ROW_SUB == SUBLANES


def _row_tile_store(ref, y):
    m = y.shape[0]
    for s in range(ROW_SUB):
        ref[pl.ds(s, m, stride=ROW_SUB), :] = y[:, s * LANES:(s + 1) * LANES]


def _row_tile_load(ref, m, s):
    return ref[pl.ds(s, m, stride=ROW_SUB), :]


def _row_of(ref, i):
    return ref.at[pl.ds(pl.multiple_of(i * ROW_SUB, ROW_SUB), ROW_SUB)]


def _start_rows(row_copy, m):
    def body(j, c):
        row_copy(j, 0).start(priority=0)
        row_copy(j, 1).start(priority=1)
        return c
    lax.fori_loop(0, m, body, 0, unroll=DMA_UNROLL)


def _wait_rows(row_copy, m):
    def body(j, c):
        row_copy(j, 0).wait()
        row_copy(j, 1).wait()
        return c
    lax.fori_loop(0, m, body, 0, unroll=DMA_UNROLL)


def _dispatch_kernel(experts_per_step, dest_ref, pad_lo_ref, pad_hi_ref, x_ref, g_ref, xs_hbm,
                     xta_ref, xtb_ref, zero_ref, sem_a, sem_b, sem_z):
    th = x_ref.shape[0] // 2
    i = pl.program_id(0)

    @pl.when(i == 0)
    def _():
        zero_ref[...] = jnp.zeros_like(zero_ref)

    for k in range(experts_per_step):
        e = i * experts_per_step + k

        @pl.when(e < MOE_EXPERTS)
        def _(e=e):
            lo, count = pad_lo_ref[e], pad_hi_ref[e] - pad_lo_ref[e]
            sizes = [BM_EXPERT >> (b + 1) for b in range(BM_EXPERT.bit_length() - 1)]
            copies, off = [], lo
            for size in sizes:
                copies.append(pltpu.make_async_copy(zero_ref.at[pl.ds(0, size)], xs_hbm.at[pl.ds(off, size)], sem_z))
                off = off + (count & size)
            for size, cp in zip(sizes, copies):
                @pl.when((count & size) != 0)
                def _(cp=cp):
                    cp.start()
            for size, cp in zip(sizes, copies):
                @pl.when((count & size) != 0)
                def _(cp=cp):
                    cp.wait()

    def normed(x):
        return x * lax.rsqrt(jnp.mean(x * x, axis=-1, keepdims=True) + EPS) * g_ref[...]

    def copy_a(j, k):
        return pltpu.make_async_copy(_row_of(xta_ref, j), xs_hbm.at[dest_ref[2 * j + k]], sem_a)

    def copy_b(j, k):
        return pltpu.make_async_copy(_row_of(xtb_ref, j), xs_hbm.at[dest_ref[2 * th + 2 * j + k]], sem_b)

    @pl.when(i > 0)
    def _():
        _wait_rows(copy_a, th)
    _row_tile_store(xta_ref, normed(x_ref[:th, :]))
    _start_rows(copy_a, th)

    @pl.when(i > 0)
    def _():
        _wait_rows(copy_b, th)
    _row_tile_store(xtb_ref, normed(x_ref[th:, :]))
    _start_rows(copy_b, th)

    @pl.when(i == pl.num_programs(0) - 1)
    def _():
        _wait_rows(copy_a, th)
        _wait_rows(copy_b, th)
        zrows = zero_ref.shape[0]
        used = pad_hi_ref[MOE_EXPERTS - 1]
        tail = [pltpu.make_async_copy(zero_ref, xs_hbm.at[pl.ds(used + t * zrows, zrows)], sem_z)
                for t in range(MOE_EXPERTS * BM_EXPERT // zrows)]
        total = xs_hbm.shape[0]
        for t, cp in enumerate(tail):
            @pl.when(used + t * zrows < total)
            def _(cp=cp):
                cp.start()
        for t, cp in enumerate(tail):
            @pl.when(used + t * zrows < total)
            def _(cp=cp):
                cp.wait()


def _dispatch(x2d, g, dest, pad_lo, pad_hi, rows):
    n, d = x2d.shape
    td = min(TD_DISPATCH, n)
    nsteps = n // td
    experts_per_step = -(-MOE_EXPERTS // nsteps)
    half_buf = pltpu.VMEM((td // 2 * ROW_SUB, LANES), F32)
    return pl.pallas_call(
        functools.partial(_dispatch_kernel, experts_per_step),
        grid=(nsteps,),
        in_specs=[pl.BlockSpec((2 * td,), lambda i: (i,), memory_space=pltpu.SMEM),
                  pl.BlockSpec(memory_space=pltpu.SMEM),
                  pl.BlockSpec(memory_space=pltpu.SMEM),
                  pl.BlockSpec((td, d), lambda i: (i, 0)),
                  pl.BlockSpec((1, d), lambda i: (0, 0))],
        out_specs=pl.BlockSpec(memory_space=pl.ANY),
        out_shape=jax.ShapeDtypeStruct((rows, ROW_SUB, LANES), F32),
        scratch_shapes=[half_buf, half_buf, pltpu.VMEM((BM_EXPERT // 2, ROW_SUB, LANES), F32),
                        pltpu.SemaphoreType.DMA, pltpu.SemaphoreType.DMA, pltpu.SemaphoreType.DMA],
        compiler_params=_cparams("arbitrary"),
        name="moe_dispatch",
    )(dest, pad_lo, pad_hi, x2d, g.reshape(1, d))


def _experts_kernel(be_ref, nused_ref, xs_ref, w1_ref, w3_ref, w2_ref, ys_ref, w1b_ref, w3b_ref, w2b_ref):
    i = pl.program_id(0)
    bm = xs_ref.shape[0] // ROW_SUB

    @pl.when(jnp.logical_or(i == 0, be_ref[i] != be_ref[jnp.maximum(i - 1, 0)]))
    def _():
        w1b_ref[...] = w1_ref[0].astype(BF16)
        w3b_ref[...] = w3_ref[0].astype(BF16)
        w2b_ref[...] = w2_ref[0].astype(BF16)

    @pl.when(i < nused_ref[0])
    def _():
        h = jnp.concatenate([_row_tile_load(xs_ref, bm, s) for s in range(ROW_SUB)], axis=1).astype(BF16)
        hid = _silu(_dot(h, w1b_ref[...])) * _dot(h, w3b_ref[...])
        _row_tile_store(ys_ref, _dot(hid.astype(BF16), w2b_ref[...]))

    @pl.when(i >= nused_ref[0])
    def _():
        ys_ref[...] = jnp.zeros_like(ys_ref)


def _experts(xs, blk_expert, n_used, w1, w3, w2):
    rows = xs.shape[0] // ROW_SUB
    d = D_MODEL
    nb = rows // BM_EXPERT
    ff = w1.shape[-1]
    blk = BM_EXPERT * ROW_SUB
    grid_spec = pltpu.PrefetchScalarGridSpec(
        num_scalar_prefetch=2,
        grid=(nb,),
        in_specs=[pl.BlockSpec((blk, LANES), lambda i, be, nu: (jnp.minimum(i, nu[0] - 1), 0)),
                  pl.BlockSpec((1, d, ff), lambda i, be, nu: (be[i], 0, 0)),
                  pl.BlockSpec((1, d, ff), lambda i, be, nu: (be[i], 0, 0)),
                  pl.BlockSpec((1, ff, d), lambda i, be, nu: (be[i], 0, 0))],
        out_specs=pl.BlockSpec((blk, LANES), lambda i, be, nu: (i, 0)),
        scratch_shapes=[pltpu.VMEM((d, ff), BF16), pltpu.VMEM((d, ff), BF16), pltpu.VMEM((ff, d), BF16)],
    )
    return pl.pallas_call(
        _experts_kernel,
        grid_spec=grid_spec,
        out_shape=jax.ShapeDtypeStruct((rows * ROW_SUB, LANES), F32),
        compiler_params=_cparams("arbitrary"),
        name="moe_experts",
    )(blk_expert, n_used, xs, w1, w3, w2)


def _combine_kernel(final, dest_ref, dest_next_ref, x_ref, meta_ref, ys_hbm, g_ref, o_ref,
                    ya0_ref, ya1_ref, yb0_ref, yb1_ref, sem_a, sem_b):
    th = x_ref.shape[0] // 2
    i = pl.program_id(0)

    def gather(dref, off, bufs, sem):
        def row_copy(j, k):
            return pltpu.make_async_copy(ys_hbm.at[dref[off + 2 * j + k]], _row_of(bufs[k], j), sem)
        return row_copy

    copy_a = gather(dest_ref, 0, (ya0_ref, ya1_ref), sem_a)
    copy_b = gather(dest_ref, 2 * th, (yb0_ref, yb1_ref), sem_b)
    copy_a_next = gather(dest_next_ref, 0, (ya0_ref, ya1_ref), sem_a)

    def combine(half, y0_ref, y1_ref):
        rows = slice(half * th, (half + 1) * th)
        meta = meta_ref[rows, :]
        gate0 = meta[:, META_GATE:META_GATE + 1]
        gate1 = meta[:, META_GATE + 1:META_GATE + 2]
        cols = [slice(s * LANES, (s + 1) * LANES) for s in range(ROW_SUB)]
        ys = [x_ref[rows, cols[s]] + gate0 * _row_tile_load(y0_ref, th, s) + gate1 * _row_tile_load(y1_ref, th, s)
              for s in range(ROW_SUB)]
        if final:
            ssq = sum(jnp.sum(y * y, axis=-1, keepdims=True) for y in ys)
            inv = lax.rsqrt(ssq * (1.0 / D_MODEL) + EPS)
            ys = [y * inv * g_ref[:, cols[s]] for s, y in enumerate(ys)]
        for s, y in enumerate(ys):
            o_ref[rows, cols[s]] = y

    @pl.when(i == 0)
    def _():
        _start_rows(copy_a, th)
    _start_rows(copy_b, th)
    _wait_rows(copy_a, th)
    combine(0, ya0_ref, ya1_ref)

    @pl.when(i + 1 < pl.num_programs(0))
    def _():
        _start_rows(copy_a_next, th)
    _wait_rows(copy_b, th)
    combine(1, yb0_ref, yb1_ref)


def _combine(x2d, meta, dest, ys, final_g, final):
    n, d = x2d.shape
    tc = min(TC_COMBINE, n)
    nsteps = n // tc
    half_buf = pltpu.VMEM((tc // 2 * ROW_SUB, LANES), F32)
    return pl.pallas_call(
        functools.partial(_combine_kernel, final),
        grid=(nsteps,),
        in_specs=[pl.BlockSpec((2 * tc,), lambda i: (i,), memory_space=pltpu.SMEM),
                  pl.BlockSpec((2 * tc,), lambda i: (jnp.minimum(i + 1, nsteps - 1),), memory_space=pltpu.SMEM),
                  pl.BlockSpec((tc, d), lambda i: (i, 0)),
                  pl.BlockSpec((tc, LANES), lambda i: (i, 0)),
                  pl.BlockSpec(memory_space=pl.ANY),
                  pl.BlockSpec((1, d), lambda i: (0, 0))],
        out_specs=pl.BlockSpec((tc, d), lambda i: (i, 0)),
        out_shape=jax.ShapeDtypeStruct((n, d), F32),
        scratch_shapes=[half_buf, half_buf, half_buf, half_buf,
                        pltpu.SemaphoreType.DMA, pltpu.SemaphoreType.DMA],
        compiler_params=_cparams("arbitrary"),
        name="moe_combine_final" if final else "moe_combine",
    )(dest, dest, x2d, meta, ys, final_g.reshape(1, d))


def _moe(x2d, norm_g, w_group, w_expert, w1, w3, w2, layer, final_g, final):
    n, d = x2d.shape
    meta, cnt = _router(x2d, norm_g, w_group, w_expert)
    counts = cnt[EXP_LANE0:EXP_LANE0 + MOE_EXPERTS, 0].astype(jnp.int32)
    padded = (counts + BM_EXPERT - 1) // BM_EXPERT * BM_EXPERT
    pad_end = jnp.cumsum(padded)
    pad_start = pad_end - padded
    e = meta[:, META_E:META_E + 2].astype(jnp.int32)
    rank = meta[:, META_RANK:META_RANK + 2].astype(jnp.int32)
    eids = jnp.arange(MOE_EXPERTS, dtype=jnp.int32)
    dest = (jnp.sum(jnp.where(e[..., None] == eids, pad_start, 0), axis=-1) + rank).reshape(2 * n)
    rows = 2 * n + MOE_EXPERTS * BM_EXPERT
    nb = rows // BM_EXPERT
    blk_start = jnp.arange(nb, dtype=jnp.int32) * BM_EXPERT
    blk_expert = jnp.minimum(jnp.sum((pad_end[None, :] <= blk_start[:, None]).astype(jnp.int32), axis=1),
                             MOE_EXPERTS - 1)
    n_used = (pad_end[-1:] // BM_EXPERT).astype(jnp.int32)
    xs = _dispatch(x2d, norm_g, dest, pad_start + counts, pad_end, rows)
    ys = _experts(xs.reshape(rows * ROW_SUB, LANES), blk_expert + layer * MOE_EXPERTS, n_used, w1, w3, w2)
    return _combine(x2d, meta, dest, ys.reshape(rows, ROW_SUB, LANES), final_g, final)


def _split_w_in(w):
    n_ab = 2 * A_WIDTH + 4 * B_WIDTH
    n_c = n_ab + 2 * B_HEADS
    n_g = n_c + 2 * 3 * C_WIDTH + C_WIDTH
    w_main = jnp.concatenate([w[:, n_g:], w[:, :n_ab]], axis=1).astype(BF16)
    w_c = w[:, n_c:n_g].astype(BF16)
    w_small = jnp.pad(w[:, n_ab:n_c], ((0, 0), (0, LANES - 2 * B_HEADS))).astype(BF16)
    return w_main, w_c, w_small


def kernel(x, norm_mix_g, w_in, a_ln_g, a_ln_b, a_w_s, a_b_s, b_conv_w, b_a_log, b_dt_bias, b_norm_g, rel_bias,
           w_branch, w_out, norm_ffn_g, w_group, w_expert, w1, w3, w2, final_norm_g):
    batch, seq, d = x.shape
    depth = w_in.shape[0]
    x2d = x.reshape(batch * seq, d)
    for l in range(depth):
        w_main, w_c, w_small = _split_w_in(w_in[l])
        assert w_main.shape[1] == PROJ_W and w_c.shape[1] == CPROJ_W
        proj, cproj, p2 = _in_proj(x2d, norm_mix_g[l], w_main, w_c, w_small)
        yb = _deltanet(proj, p2, b_conv_w[l], b_a_log[l], b_dt_bias[l], b_norm_g[l], batch, seq)
        yc = _dilated_attn(cproj, rel_bias, batch, seq)
        x2d = _merge(x2d, proj, a_w_s[l], a_b_s[l], a_ln_g[l], a_ln_b[l], yb, yc, w_branch[l], w_out[l])
        x2d = _moe(x2d, norm_ffn_g[l], w_group[l], w_expert[l], w1.reshape((-1,) + w1.shape[2:]),
                   w3.reshape((-1,) + w3.shape[2:]), w2.reshape((-1,) + w2.shape[2:]), l, final_norm_g,
                   final=(l == depth - 1))
    return x2d.reshape(batch, seq, d)
```

```python
import functools
import math

import numpy as np
import jax
import jax.numpy as jnp
from jax import lax
from jax.experimental import pallas as pl
from jax.experimental.pallas import tpu as pltpu

F32 = jnp.float32
BF16 = jnp.bfloat16

D_MODEL = 1024
A_GROUPS, A_CHUNK, A_WIDTH = 4, 128, 512
B_HEADS, B_HEAD_DIM, B_CONV, B_WIDTH = 4, 128, 4, 512
C_PATTERNS = ((128, 1), (512, 4), (2048, 16))
C_HEADS, C_HEAD_DIM, C_BLOCK, C_WIDTH = 4, 128, 128, 512
REL_BUCKETS, REL_MAX_DIST = 32, 2048
MOE_GROUPS, MOE_PER_GROUP, MOE_EXPERTS, MOE_FF = 4, 8, 32, 512
EPS = 1e-6

LANES = 128
SUBLANES = 8
VMEM_LIMIT = 48 * 1024 * 1024

PROJ_W = 6144
GATE_OFF, AU_OFF, AV_OFF = 0, 3072, 3584
BQ_OFF, BK_OFF, BV_OFF, BZ_OFF = 4096, 4608, 5120, 5632
WBLK = 512
CPROJ_W = 3584
CQ_OFF, CK_OFF, CV_OFF = 0, 1536, 3072

TM_PROJ, TN_PROJ = 2048, 512
TB_DELTA = 512
CHUNK = 128
TT_ATTN = 2048
TM_MERGE = 512
TM_ROUTER = 1024
TD_DISPATCH = 512
BM_EXPERT = 512
TC_COMBINE = 512


def _cparams(*sem):
    return pltpu.CompilerParams(dimension_semantics=sem, vmem_limit_bytes=VMEM_LIMIT)


def _dot(a, b):
    return jnp.dot(a, b, preferred_element_type=F32)


def _dot_nt(a, b):
    return lax.dot_general(a, b, (((1,), (1,)), ((), ())), preferred_element_type=F32)


def _sigmoid(x):
    return 1.0 / (1.0 + jnp.exp(-x))


def _silu(x):
    return x * _sigmoid(x)


def _gelu_tanh(x):
    c = math.sqrt(2.0 / math.pi)
    return x * (0.5 * (1.0 + jnp.tanh(c * (x + 0.044715 * (x * x * x)))))


def _softplus(x):
    return jnp.maximum(x, 0.0) + jnp.log(1.0 + jnp.exp(-jnp.abs(x)))


def _in_proj_kernel(nm, nc, x_ref, g_ref, wm_ref, wc_ref, ws_ref, proj_ref, cproj_ref, p2_ref, h_ref):
    j = pl.program_id(1)

    @pl.when(j == 0)
    def _():
        x = x_ref[...]
        ms = jnp.mean(x * x, axis=-1, keepdims=True)
        h_ref[...] = (x * lax.rsqrt(ms + EPS) * g_ref[...]).astype(h_ref.dtype)

    @pl.when(j < nm)
    def _():
        proj_ref[...] = _dot(h_ref[...], wm_ref[...]).astype(proj_ref.dtype)

    @pl.when(jnp.logical_and(j >= nm, j < nm + nc))
    def _():
        cproj_ref[...] = _dot(h_ref[...], wc_ref[...])

    @pl.when(j == nm + nc)
    def _():
        p2_ref[...] = _dot(h_ref[...], ws_ref[...])


def _in_proj(x2d, g, w_main, w_c, w_small):
    n, d = x2d.shape
    tm = min(TM_PROJ, n)
    tn = tn_main = TN_PROJ
    nm, nc = w_main.shape[1] // tn_main, w_c.shape[1] // tn
    main_col = lambda i, j: jnp.minimum(j, nm - 1)
    c_col = lambda i, j: jnp.clip(j - nm, 0, nc - 1)
    return pl.pallas_call(
        functools.partial(_in_proj_kernel, nm, nc),
        grid=(n // tm, nm + nc + 1),
        in_specs=[pl.BlockSpec((tm, d), lambda i, j: (i, 0)),
                  pl.BlockSpec((1, d), lambda i, j: (0, 0)),
                  pl.BlockSpec((d, tn_main), lambda i, j: (0, main_col(i, j))),
                  pl.BlockSpec((d, tn), lambda i, j: (0, c_col(i, j))),
                  pl.BlockSpec((d, LANES), lambda i, j: (0, 0))],
        out_specs=[pl.BlockSpec((tm, tn_main), lambda i, j: (i, main_col(i, j))),
                   pl.BlockSpec((tm, tn), lambda i, j: (i, c_col(i, j))),
                   pl.BlockSpec((tm, LANES), lambda i, j: (i, 0))],
        out_shape=[jax.ShapeDtypeStruct((n, w_main.shape[1]), BF16),
                   jax.ShapeDtypeStruct((n, w_c.shape[1]), F32),
                   jax.ShapeDtypeStruct((n, LANES), F32)],
        scratch_shapes=[pltpu.VMEM((tm, d), BF16)],
        compiler_params=_cparams("parallel", "arbitrary"),
        name="in_proj",
    )(x2d, g.reshape(1, d), w_main, w_c, w_small)


def _gmlp_kernel(u_ref, v_ref, w_ref, b_ref, lng_ref, lnb_ref, o_ref):
    tm = u_ref.shape[0]
    u = _gelu_tanh(u_ref[...].astype(F32))
    v = _gelu_tanh(v_ref[...].astype(F32))
    mu = jnp.mean(v, axis=-1, keepdims=True)
    vc = v - mu
    var = jnp.mean(vc * vc, axis=-1, keepdims=True)
    vn = (vc * lax.rsqrt(var + EPS) * lng_ref[...] + lnb_ref[...]).astype(BF16)
    for c in range(tm // A_CHUNK):
        rows = slice(c * A_CHUNK, (c + 1) * A_CHUNK)
        for g in range(A_GROUPS):
            cols = slice(g * LANES, (g + 1) * LANES)
            sv = _dot(w_ref[g], vn[rows, cols]) + b_ref[:, cols]
            o_ref[rows, cols] = (u[rows, cols] * sv).astype(o_ref.dtype)


def _deltanet_kernel(q_ref, k_ref, v_ref, z_ref, p2_ref, cw_ref, alog_ref, dtb_ref, ng_ref, o_ref,
                     s_ref, carry_ref, qkv_ref, gc_ref, gct_ref, beta_ref, r_ref, p_ref, stack_ref, attn_ref,
                     rhs_ref, rhsb_ref, u_ref, wq_ref, kt_ref, eg_ref, vn_ref, op_ref):
    tb = q_ref.shape[0]
    nchunk = tb // CHUNK

    @pl.when(pl.program_id(1) == 0)
    def _():
        s_ref[...] = jnp.zeros_like(s_ref)
        carry_ref[...] = jnp.zeros_like(carry_ref)

    row8 = lax.broadcasted_iota(jnp.int32, (SUBLANES, B_WIDTH), 0)
    for idx, ref in enumerate((q_ref, k_ref, v_ref)):
        x = ref[...].astype(F32)
        w = cw_ref[:, idx * B_WIDTH:(idx + 1) * B_WIDTH]
        prev = carry_ref[idx]
        acc = x * w[B_CONV - 1:B_CONV]
        for s in range(1, B_CONV):
            xs = pltpu.roll(x, s, axis=0)
            ps = pltpu.roll(prev, s, axis=0)
            head = jnp.where(row8 < s, ps, xs[:SUBLANES])
            xs = jnp.concatenate([head, xs[SUBLANES:]], axis=0)
            acc = acc + xs * w[B_CONV - 1 - s:B_CONV - s]
        carry_ref[idx] = x[tb - SUBLANES:]
        qkv_ref[idx] = _silu(acc)

    ri = lax.broadcasted_iota(jnp.int32, (CHUNK, CHUNK), 0)
    ci = lax.broadcasted_iota(jnp.int32, (CHUNK, CHUNK), 1)
    incl = ri >= ci
    strict = ri > ci
    tril_f = incl.astype(F32)
    alog = alog_ref[...]
    dtb = dtb_ref[...]
    ng = ng_ref[...]
    lo, hi = slice(0, CHUNK), slice(CHUNK, 2 * CHUNK)
    items = [(c, h) for c in range(nchunk) for h in range(B_HEADS)]

    for c in range(nchunk):
        rows = slice(c * CHUNK, (c + 1) * CHUNK)
        p2 = p2_ref[rows, :]
        g_all = -jnp.exp(alog) * _softplus(p2 + dtb)
        gc_all = jnp.dot(tril_f, g_all, preferred_element_type=F32, precision=lax.Precision.HIGHEST)
        gc_ref[c] = gc_all
        gct_ref[c] = gc_all.T
        beta_ref[c] = _sigmoid(p2)

    for i, (c, h) in enumerate(items):
        rows = slice(c * CHUNK, (c + 1) * CHUNK)
        cols = slice(h * B_HEAD_DIM, (h + 1) * B_HEAD_DIM)
        q = qkv_ref[0, rows, cols]
        k = qkv_ref[1, rows, cols]
        v = qkv_ref[2, rows, cols]
        q = q * (lax.rsqrt(jnp.sum(q * q, axis=-1, keepdims=True) + EPS) * (B_HEAD_DIM ** -0.5))
        k = k * lax.rsqrt(jnp.sum(k * k, axis=-1, keepdims=True) + EPS)
        beta = beta_ref[c, :, h:h + 1]
        gc_col = jnp.broadcast_to(gc_ref[c, :, B_HEADS + h:B_HEADS + h + 1], (CHUNK, B_HEAD_DIM))
        gc_row = gct_ref[c, B_HEADS + h:B_HEADS + h + 1, :]
        decay = jnp.exp(jnp.where(incl, gc_col - gc_row, -jnp.inf))
        kb = k * beta
        vb = v * beta
        egc = jnp.exp(gc_col)
        aq = _dot_nt(jnp.concatenate([kb, q], axis=0).astype(BF16), k.astype(BF16))
        m = -jnp.where(strict, aq[lo] * decay, 0.0)
        r_ref[i] = m
        stack_ref[i, lo] = m.astype(BF16)
        attn_ref[i] = (aq[hi] * decay).astype(BF16)
        rhs = jnp.concatenate([vb, kb * egc], axis=1)
        rhs_ref[i] = rhs
        rhsb_ref[i] = rhs.astype(BF16)
        wq_ref[i, hi] = (q * egc).astype(BF16)
        g_last = gc_col[CHUNK - 1:CHUNK, :]
        kt_ref[i] = (k * jnp.exp(g_last - gc_col)).T.astype(BF16)
        eg_ref[i] = jnp.exp(g_last)

    for i in range(len(items)):
        mb = stack_ref[i, lo]
        p = _dot(mb, mb)
        p_ref[i] = p
        stack_ref[i, hi] = p.astype(BF16)
    for _ in range(5):
        for i in range(len(items)):
            rp = _dot(stack_ref[i], stack_ref[i, hi])
            r = r_ref[i] + p_ref[i] + rp[lo]
            p = rp[hi]
            r_ref[i] = r
            p_ref[i] = p
            stack_ref[i, lo] = r.astype(BF16)
            stack_ref[i, hi] = p.astype(BF16)
    for i in range(len(items)):
        r = r_ref[i] + p_ref[i] + _dot(stack_ref[i, lo], stack_ref[i, hi])
        sol = rhs_ref[i] + _dot(r.astype(BF16), rhsb_ref[i])
        u_ref[i] = sol[:, :B_HEAD_DIM]
        wq_ref[i, lo] = sol[:, B_HEAD_DIM:].astype(BF16)

    for c in range(nchunk):
        rows = slice(c * CHUNK, (c + 1) * CHUNK)
        for h in range(B_HEADS):
            i = c * B_HEADS + h
            wq = _dot(wq_ref[i], s_ref[h].astype(BF16))
            vn_ref[h] = (u_ref[i] - wq[lo]).astype(BF16)
            op_ref[h] = wq[hi]
        for h in range(B_HEADS):
            i = c * B_HEADS + h
            cols = slice(h * B_HEAD_DIM, (h + 1) * B_HEAD_DIM)
            vnb = vn_ref[h]
            o = op_ref[h] + _dot(attn_ref[i], vnb)
            s_ref[h] = s_ref[h] * eg_ref[i] + _dot(kt_ref[i], vnb)
            z = z_ref[rows, cols].astype(F32)
            on = o * lax.rsqrt(jnp.mean(o * o, axis=-1, keepdims=True) + EPS) * ng
            o_ref[rows, cols] = (on * _silu(z)).astype(o_ref.dtype)


def _deltanet(proj, p2, conv_w, a_log, dt_bias, norm_g, batch, seq):
    tb = min(TB_DELTA, seq)
    nt = seq // tb
    nchunk = tb // CHUNK
    ni = nchunk * B_HEADS
    dk = B_HEAD_DIM

    def col(off):
        return pl.BlockSpec((tb, WBLK), lambda b, t: (b * nt + t, off // WBLK))

    pad = LANES - 2 * B_HEADS
    alog = jnp.pad(a_log.astype(F32), (B_HEADS, pad)).reshape(1, LANES)
    dtb = jnp.pad(dt_bias.astype(F32), (B_HEADS, pad)).reshape(1, LANES)
    return pl.pallas_call(
        _deltanet_kernel,
        grid=(batch, nt),
        in_specs=[col(BQ_OFF), col(BK_OFF), col(BV_OFF), col(BZ_OFF),
                  pl.BlockSpec((tb, LANES), lambda b, t: (b * nt + t, 0)),
                  pl.BlockSpec((B_CONV, 3 * B_WIDTH), lambda b, t: (0, 0)),
                  pl.BlockSpec((1, LANES), lambda b, t: (0, 0)),
                  pl.BlockSpec((1, LANES), lambda b, t: (0, 0)),
                  pl.BlockSpec((1, B_HEAD_DIM), lambda b, t: (0, 0))],
        out_specs=pl.BlockSpec((tb, B_WIDTH), lambda b, t: (b * nt + t, 0)),
        out_shape=jax.ShapeDtypeStruct((batch * seq, B_WIDTH), BF16),
        scratch_shapes=[pltpu.VMEM((B_HEADS, dk, dk), F32),
                        pltpu.VMEM((3, SUBLANES, B_WIDTH), F32),
                        pltpu.VMEM((3, tb, B_WIDTH), F32),
                        pltpu.VMEM((nchunk, CHUNK, LANES), F32),
                        pltpu.VMEM((nchunk, LANES, CHUNK), F32),
                        pltpu.VMEM((nchunk, CHUNK, LANES), F32),
                        pltpu.VMEM((ni, CHUNK, CHUNK), F32),
                        pltpu.VMEM((ni, CHUNK, CHUNK), F32),
                        pltpu.VMEM((ni, 2 * CHUNK, CHUNK), BF16),
                        pltpu.VMEM((ni, CHUNK, CHUNK), BF16),
                        pltpu.VMEM((ni, CHUNK, 2 * dk), F32),
                        pltpu.VMEM((ni, CHUNK, 2 * dk), BF16),
                        pltpu.VMEM((ni, CHUNK, dk), F32),
                        pltpu.VMEM((ni, 2 * CHUNK, dk), BF16),
                        pltpu.VMEM((ni, dk, CHUNK), BF16),
                        pltpu.VMEM((ni, 1, dk), F32),
                        pltpu.VMEM((B_HEADS, CHUNK, dk), BF16),
                        pltpu.VMEM((B_HEADS, CHUNK, dk), F32)],
        compiler_params=_cparams("parallel", "arbitrary"),
        name="deltanet",
    )(proj, proj, proj, proj, p2, conv_w.astype(F32), alog, dtb, norm_g.reshape(1, -1).astype(F32))


def _t5_bucket(dist):
    max_exact = REL_BUCKETS // 2
    n = np.maximum(dist, 0)
    large = max_exact + (np.log(np.maximum(n, 1) / max_exact) / math.log(REL_MAX_DIST / max_exact)
                         * (REL_BUCKETS - max_exact)).astype(np.int32)
    large = np.minimum(large, REL_BUCKETS - 1)
    return np.where(n < max_exact, n, large).astype(np.int32)


def _band_pattern(window, dil):
    i = np.arange(C_BLOCK)[:, None]
    kk = np.arange(2 * C_BLOCK)[None, :]
    j = C_BLOCK + i - kk
    band = (j >= 0) & (j <= window // dil)
    bucket = _t5_bucket(np.clip(j, 0, None) * dil)
    return band, bucket


def _attn_kernel(q0_ref, q1_ref, q2_ref, k0_ref, k1_ref, k2_ref, kp0_ref, kp1_ref, kp2_ref, v_ref, vp_ref,
                 bias_ref, o_ref, s_ref, p_ref, inv_ref, og_ref, lse_ref):
    tt = q0_ref.shape[0]
    first = pl.program_id(1) == 0
    q_refs = (q0_ref, q1_ref, q2_ref)
    k_refs = (k0_ref, k1_ref, k2_ref)
    kp_refs = (kp0_ref, kp1_ref, kp2_ref)
    scale = C_HEAD_DIM ** -0.5
    shape = (C_BLOCK, C_HEAD_DIM)
    lo, hi = slice(0, C_BLOCK), slice(C_BLOCK, 2 * C_BLOCK)

    for g, (_, dil) in enumerate(C_PATTERNS):
        q_ref, k_ref, kp_ref = q_refs[g], k_refs[g], kp_refs[g]
        span = C_BLOCK * dil
        nblk = tt // span
        def rows(start, dil=dil):
            return pl.ds(start, C_BLOCK, stride=dil) if dil > 1 else pl.ds(start, C_BLOCK)

        blocks = [(rows(r + n * span), rows(r + (n - 1) * span) if n else rows(r + (nblk - 1) * span), n == 0)
                  for r in range(dil) for n in range(nblk)]

        for j, (sl, psl, from_prev_tile) in enumerate(blocks):
            q = (q_ref[sl, :] * scale).astype(BF16)
            kp = (kp_ref if from_prev_tile else k_ref)[psl, :].astype(BF16)
            s_p = _dot_nt(q, kp) + bias_ref[g, 0, :, lo]
            if from_prev_tile:
                s_p = jnp.where(first, -jnp.inf, s_p)
            s_ref[j, :, lo] = s_p
            s_ref[j, :, hi] = _dot_nt(q, k_ref[sl, :].astype(BF16)) + bias_ref[g, 0, :, hi]

        for j, (sl, psl, from_prev_tile) in enumerate(blocks):
            s = s_ref[j]
            mx = jnp.max(s, axis=-1, keepdims=True)
            p = jnp.exp(s - mx)
            den = jnp.sum(p, axis=-1, keepdims=True)
            p_ref[j] = p.astype(BF16)
            inv_ref[j] = jnp.broadcast_to(1.0 / den, shape)
            lse_ref[g, sl, :] = jnp.broadcast_to(mx + jnp.log(den), shape)

        for j, (sl, psl, from_prev_tile) in enumerate(blocks):
            vp = (vp_ref if from_prev_tile else v_ref)[psl, :].astype(BF16)
            num = _dot(p_ref[j, :, lo], vp) + _dot(p_ref[j, :, hi], v_ref[sl, :].astype(BF16))
            og_ref[g, sl, :] = num * inv_ref[j]

    piece = 2 * C_BLOCK
    for i in range(tt // piece):
        sl = slice(i * piece, (i + 1) * piece)
        l0, l1, l2 = lse_ref[0, sl, :], lse_ref[1, sl, :], lse_ref[2, sl, :]
        lm = jnp.maximum(jnp.maximum(l0, l1), l2)
        e0, e1, e2 = jnp.exp(l0 - lm), jnp.exp(l1 - lm), jnp.exp(l2 - lm)
        num = e0 * og_ref[0, sl, :] + e1 * og_ref[1, sl, :] + e2 * og_ref[2, sl, :]
        o_ref[sl, :] = (num / (e0 + e1 + e2)).astype(o_ref.dtype)


def _dilated_attn(cproj, rel_bias, batch, seq):
    tt = TT_ATTN
    assert seq % tt == 0 and tt == C_BLOCK * max(d for _, d in C_PATTERNS)
    nt = seq // tt
    biases = []
    for gi, (window, dil) in enumerate(C_PATTERNS):
        band, bucket = _band_pattern(window, dil)
        onehot = jnp.asarray(bucket[..., None] == np.arange(REL_BUCKETS), F32)
        table = rel_bias[:, gi * C_HEADS:(gi + 1) * C_HEADS].astype(F32)
        bias = jnp.einsum('qkn,nh->hqk', onehot, table, precision=lax.Precision.HIGHEST)
        biases.append(jnp.where(band[None], bias, -jnp.inf))
    bias = jnp.stack(biases)

    def cur(off, g=0):
        c0 = off // C_HEAD_DIM + g * C_HEADS
        return pl.BlockSpec((tt, C_HEAD_DIM), lambda b, t, h: (b * nt + t, c0 + h))

    def prev(off, g=0):
        c0 = off // C_HEAD_DIM + g * C_HEADS
        return pl.BlockSpec((tt, C_HEAD_DIM), lambda b, t, h: (b * nt + jnp.maximum(t - 1, 0), c0 + h))

    ng = len(C_PATTERNS)
    nblocks = tt // C_BLOCK
    in_specs = ([cur(CQ_OFF, g) for g in range(ng)] + [cur(CK_OFF, g) for g in range(ng)]
                + [prev(CK_OFF, g) for g in range(ng)] + [cur(CV_OFF), prev(CV_OFF)]
                + [pl.BlockSpec((ng, 1, C_BLOCK, 2 * C_BLOCK), lambda b, t, h: (0, h, 0, 0))])
    return pl.pallas_call(
        _attn_kernel,
        grid=(batch, nt, C_HEADS),
        in_specs=in_specs,
        out_specs=pl.BlockSpec((tt, C_HEAD_DIM), lambda b, t, h: (b * nt + t, h)),
        out_shape=jax.ShapeDtypeStruct((batch * seq, C_WIDTH), BF16),
        scratch_shapes=[pltpu.VMEM((nblocks, C_BLOCK, 2 * C_BLOCK), F32),
                        pltpu.VMEM((nblocks, C_BLOCK, 2 * C_BLOCK), BF16),
                        pltpu.VMEM((nblocks, C_BLOCK, C_HEAD_DIM), F32),
                        pltpu.VMEM((ng, tt, C_HEAD_DIM), F32),
                        pltpu.VMEM((ng, tt, C_HEAD_DIM), F32)],
        compiler_params=_cparams("parallel", "arbitrary", "arbitrary"),
        name="dilated_attn",
    )(*([cproj] * (3 * ng + 2)), bias)


def _merge_kernel(x_ref, g0_ref, g1_ref, g2_ref, au_ref, av_ref, ws_ref, bs_ref, lng_ref, lnb_ref, yb_ref, yc_ref,
                  wb_ref, wo_ref, out_ref, ya_ref):
    _gmlp_kernel(au_ref, av_ref, ws_ref, bs_ref, lng_ref, lnb_ref, ya_ref)
    merged = (_sigmoid(g0_ref[...].astype(F32)) * _dot(ya_ref[...], wb_ref[0])
              + _sigmoid(g1_ref[...].astype(F32)) * _dot(yb_ref[...], wb_ref[1])
              + _sigmoid(g2_ref[...].astype(F32)) * _dot(yc_ref[...], wb_ref[2]))
    out_ref[...] = x_ref[...] + _dot(merged.astype(BF16), wo_ref[...])


def _merge(x2d, proj, a_w_s, a_b_s, a_ln_g, a_ln_b, yb, yc, w_branch, w_out):
    n, d = x2d.shape
    tm = min(TM_MERGE, n)
    assert tm % A_CHUNK == 0
    row = lambda w: pl.BlockSpec((tm, w), lambda i: (i, 0))
    gate = lambda g: pl.BlockSpec((tm, d), lambda i: (i, GATE_OFF // d + g))
    full = lambda shape: pl.BlockSpec(shape, lambda i: (0,) * len(shape))
    causal = np.tril(np.ones((A_CHUNK, A_CHUNK), dtype=bool))
    w_s = jnp.where(causal, a_w_s, 0.0).astype(BF16)
    bias = jnp.repeat(jnp.transpose(a_b_s), LANES, axis=1).astype(F32)
    return pl.pallas_call(
        _merge_kernel,
        grid=(n // tm,),
        in_specs=[row(d), gate(0), gate(1), gate(2),
                  pl.BlockSpec((tm, WBLK), lambda i: (i, AU_OFF // WBLK)),
                  pl.BlockSpec((tm, WBLK), lambda i: (i, AV_OFF // WBLK)),
                  full((A_GROUPS, A_CHUNK, A_CHUNK)), full((A_CHUNK, A_WIDTH)), full((1, A_WIDTH)), full((1, A_WIDTH)),
                  row(WBLK), row(WBLK), full((3, WBLK, d)), full((d, d))],
        out_specs=row(d),
        out_shape=jax.ShapeDtypeStruct((n, d), F32),
        scratch_shapes=[pltpu.VMEM((tm, A_WIDTH), BF16)],
        compiler_params=_cparams("parallel"),
        name="merge",
    )(x2d, proj, proj, proj, proj, proj, w_s, bias, a_ln_g.reshape(1, -1), a_ln_b.reshape(1, -1), yb, yc,
      w_branch.astype(BF16), w_out.astype(BF16))


EXP_LANE0 = MOE_GROUPS
META_E, META_RANK, META_GATE = 0, 2, 4


ROUTER_ROWS = 40


def _router_kernel(x_ref, g_ref, whi_ref, wlo_ref, meta_ref, cnt_ref, run_ref):
    tm = x_ref.shape[0]
    nr = ROUTER_ROWS

    @pl.when(pl.program_id(0) == 0)
    def _():
        run_ref[...] = jnp.zeros_like(run_ref)

    x = x_ref[...]
    h = x * lax.rsqrt(jnp.mean(x * x, axis=-1, keepdims=True) + EPS) * g_ref[...]
    h_hi = h.astype(BF16)
    h_lo = (h - h_hi.astype(F32)).astype(BF16)
    w_hi = whi_ref[...]
    logits = (_dot_nt(w_hi, h_hi) + (_dot_nt(w_hi, h_lo) + _dot_nt(wlo_ref[...], h_hi)))[:nr]
    row_f = lax.broadcasted_iota(jnp.int32, (nr, tm), 0).astype(F32)
    big = float(LANES)

    def first_argmax(vals, mx):
        return jnp.min(jnp.where(vals == mx, row_f, big), axis=0, keepdims=True)

    gl = jnp.where(row_f < MOE_GROUPS, logits, -jnp.inf)
    gmax = jnp.max(gl, axis=0, keepdims=True)
    grp_p = 1.0 / jnp.sum(jnp.exp(gl - gmax), axis=0, keepdims=True)
    gidx = first_argmax(gl, gmax)
    lo = EXP_LANE0 + gidx * MOE_PER_GROUP
    in_grp = (row_f >= lo) & (row_f < lo + MOE_PER_GROUP)
    el = jnp.where(in_grp, logits, -jnp.inf)
    m1 = jnp.max(el, axis=0, keepdims=True)
    i1 = first_argmax(el, m1)
    el2 = jnp.where(row_f == i1, -jnp.inf, el)
    m2 = jnp.max(el2, axis=0, keepdims=True)
    i2 = first_argmax(el2, m2)
    t2 = jnp.exp(m2 - m1)
    gate1 = grp_p / (1.0 + t2)
    gate2 = grp_p * t2 / (1.0 + t2)
    oh1 = row_f == i1
    oh2 = row_f == i2
    onehot = jnp.where(oh1 | oh2, 1.0, 0.0)
    si = lax.broadcasted_iota(jnp.int32, (tm, tm), 0)
    ti = lax.broadcasted_iota(jnp.int32, (tm, tm), 1)
    before = jnp.where(si < ti, 1.0, 0.0).astype(BF16)
    prior = _dot(onehot.astype(BF16), before) + run_ref[:, 0:1]
    rank1 = jnp.sum(jnp.where(oh1, prior, 0.0), axis=0, keepdims=True)
    rank2 = jnp.sum(jnp.where(oh2, prior, 0.0), axis=0, keepdims=True)
    run_new = jnp.broadcast_to((prior + onehot)[:, tm - 1:tm], (nr, LANES))
    run_ref[...] = run_new
    cnt_ref[...] = run_new
    zrow = jnp.zeros((1, tm), F32)
    rec = jnp.concatenate([i1 - EXP_LANE0, i2 - EXP_LANE0, rank1, rank2, gate1, gate2, zrow, zrow]
                          + [jnp.zeros((LANES - SUBLANES, tm), F32)], axis=0)
    meta_ref[...] = rec.T


def _router(x2d, g, w_group, w_expert):
    n, d = x2d.shape
    tm = min(TM_ROUTER, n)
    w = jnp.transpose(jnp.concatenate([w_group, w_expert], axis=1).astype(F32))
    w = jnp.pad(w, ((0, LANES - w.shape[0]), (0, 0)))
    w_hi = w.astype(BF16)
    w_lo = (w - w_hi.astype(F32)).astype(BF16)
    return pl.pallas_call(
        _router_kernel,
        grid=(n // tm,),
        in_specs=[pl.BlockSpec((tm, d), lambda i: (i, 0)),
                  pl.BlockSpec((1, d), lambda i: (0, 0)),
                  pl.BlockSpec((LANES, d), lambda i: (0, 0)),
                  pl.BlockSpec((LANES, d), lambda i: (0, 0))],
        out_specs=[pl.BlockSpec((tm, LANES), lambda i: (i, 0)),
                   pl.BlockSpec((ROUTER_ROWS, LANES), lambda i: (0, 0))],
        out_shape=[jax.ShapeDtypeStruct((n, LANES), F32),
                   jax.ShapeDtypeStruct((ROUTER_ROWS, LANES), F32)],
        scratch_shapes=[pltpu.VMEM((ROUTER_ROWS, LANES), F32)],
        compiler_params=_cparams("arbitrary"),
        name="router",
    )(x2d, g.reshape(1, d), w_hi, w_lo)


DMA_UNROLL = 32


ROW_SUB = D_MODEL // LANES
assert ROW_SUB == SUBLANES


def _row_tile_store(ref, y):
    m = y.shape[0]
    for s in range(ROW_SUB):
        ref[pl.ds(s, m, stride=ROW_SUB), :] = y[:, s * LANES:(s + 1) * LANES]


def _row_tile_load(ref, m, s):
    return ref[pl.ds(s, m, stride=ROW_SUB), :]


def _row_of(ref, i):
    return ref.at[pl.ds(pl.multiple_of(i * ROW_SUB, ROW_SUB), ROW_SUB)]


def _start_rows(row_copy, m):
    def body(j, c):
        row_copy(j, 0).start(priority=0)
        row_copy(j, 1).start(priority=1)
        return c
    lax.fori_loop(0, m, body, 0, unroll=DMA_UNROLL)


def _wait_rows(row_copy, m):
    def body(j, c):
        row_copy(j, 0).wait()
        row_copy(j, 1).wait()
        return c
    lax.fori_loop(0, m, body, 0, unroll=DMA_UNROLL)


def _dispatch_kernel(experts_per_step, dest_ref, pad_lo_ref, pad_hi_ref, x_ref, g_ref, xs_hbm,
                     xta_ref, xtb_ref, zero_ref, sem_a, sem_b, sem_z):
    th = x_ref.shape[0] // 2
    i = pl.program_id(0)

    @pl.when(i == 0)
    def _():
        zero_ref[...] = jnp.zeros_like(zero_ref)

    for k in range(experts_per_step):
        e = i * experts_per_step + k

        @pl.when(e < MOE_EXPERTS)
        def _(e=e):
            lo, count = pad_lo_ref[e], pad_hi_ref[e] - pad_lo_ref[e]
            sizes = [BM_EXPERT >> (b + 1) for b in range(BM_EXPERT.bit_length() - 1)]
            copies, off = [], lo
            for size in sizes:
                copies.append(pltpu.make_async_copy(zero_ref.at[pl.ds(0, size)], xs_hbm.at[pl.ds(off, size)], sem_z))
                off = off + (count & size)
            for size, cp in zip(sizes, copies):
                @pl.when((count & size) != 0)
                def _(cp=cp):
                    cp.start()
            for size, cp in zip(sizes, copies):
                @pl.when((count & size) != 0)
                def _(cp=cp):
                    cp.wait()

    def normed(x):
        return x * lax.rsqrt(jnp.mean(x * x, axis=-1, keepdims=True) + EPS) * g_ref[...]

    def copy_a(j, k):
        return pltpu.make_async_copy(_row_of(xta_ref, j), xs_hbm.at[dest_ref[2 * j + k]], sem_a)

    def copy_b(j, k):
        return pltpu.make_async_copy(_row_of(xtb_ref, j), xs_hbm.at[dest_ref[2 * th + 2 * j + k]], sem_b)

    @pl.when(i > 0)
    def _():
        _wait_rows(copy_a, th)
    _row_tile_store(xta_ref, normed(x_ref[:th, :]))
    _start_rows(copy_a, th)

    @pl.when(i > 0)
    def _():
        _wait_rows(copy_b, th)
    _row_tile_store(xtb_ref, normed(x_ref[th:, :]))
    _start_rows(copy_b, th)

    @pl.when(i == pl.num_programs(0) - 1)
    def _():
        _wait_rows(copy_a, th)
        _wait_rows(copy_b, th)
        zrows = zero_ref.shape[0]
        used = pad_hi_ref[MOE_EXPERTS - 1]
        tail = [pltpu.make_async_copy(zero_ref, xs_hbm.at[pl.ds(used + t * zrows, zrows)], sem_z)
                for t in range(MOE_EXPERTS * BM_EXPERT // zrows)]
        total = xs_hbm.shape[0]
        for t, cp in enumerate(tail):
            @pl.when(used + t * zrows < total)
            def _(cp=cp):
                cp.start()
        for t, cp in enumerate(tail):
            @pl.when(used + t * zrows < total)
            def _(cp=cp):
                cp.wait()


def _dispatch(x2d, g, dest, pad_lo, pad_hi, rows):
    n, d = x2d.shape
    td = min(TD_DISPATCH, n)
    nsteps = n // td
    experts_per_step = -(-MOE_EXPERTS // nsteps)
    half_buf = pltpu.VMEM((td // 2 * ROW_SUB, LANES), F32)
    return pl.pallas_call(
        functools.partial(_dispatch_kernel, experts_per_step),
        grid=(nsteps,),
        in_specs=[pl.BlockSpec((2 * td,), lambda i: (i,), memory_space=pltpu.SMEM),
                  pl.BlockSpec(memory_space=pltpu.SMEM),
                  pl.BlockSpec(memory_space=pltpu.SMEM),
                  pl.BlockSpec((td, d), lambda i: (i, 0)),
                  pl.BlockSpec((1, d), lambda i: (0, 0))],
        out_specs=pl.BlockSpec(memory_space=pl.ANY),
        out_shape=jax.ShapeDtypeStruct((rows, ROW_SUB, LANES), F32),
        scratch_shapes=[half_buf, half_buf, pltpu.VMEM((BM_EXPERT // 2, ROW_SUB, LANES), F32),
                        pltpu.SemaphoreType.DMA, pltpu.SemaphoreType.DMA, pltpu.SemaphoreType.DMA],
        compiler_params=_cparams("arbitrary"),
        name="moe_dispatch",
    )(dest, pad_lo, pad_hi, x2d, g.reshape(1, d))


def _experts_kernel(be_ref, nused_ref, xs_ref, w1_ref, w3_ref, w2_ref, ys_ref, w1b_ref, w3b_ref, w2b_ref):
    i = pl.program_id(0)
    bm = xs_ref.shape[0] // ROW_SUB

    @pl.when(jnp.logical_or(i == 0, be_ref[i] != be_ref[jnp.maximum(i - 1, 0)]))
    def _():
        w1b_ref[...] = w1_ref[0].astype(BF16)
        w3b_ref[...] = w3_ref[0].astype(BF16)
        w2b_ref[...] = w2_ref[0].astype(BF16)

    @pl.when(i < nused_ref[0])
    def _():
        h = jnp.concatenate([_row_tile_load(xs_ref, bm, s) for s in range(ROW_SUB)], axis=1).astype(BF16)
        hid = _silu(_dot(h, w1b_ref[...])) * _dot(h, w3b_ref[...])
        _row_tile_store(ys_ref, _dot(hid.astype(BF16), w2b_ref[...]))

    @pl.when(i >= nused_ref[0])
    def _():
        ys_ref[...] = jnp.zeros_like(ys_ref)


def _experts(xs, blk_expert, n_used, w1, w3, w2):
    rows = xs.shape[0] // ROW_SUB
    d = D_MODEL
    nb = rows // BM_EXPERT
    ff = w1.shape[-1]
    blk = BM_EXPERT * ROW_SUB
    grid_spec = pltpu.PrefetchScalarGridSpec(
        num_scalar_prefetch=2,
        grid=(nb,),
        in_specs=[pl.BlockSpec((blk, LANES), lambda i, be, nu: (jnp.minimum(i, nu[0] - 1), 0)),
                  pl.BlockSpec((1, d, ff), lambda i, be, nu: (be[i], 0, 0)),
                  pl.BlockSpec((1, d, ff), lambda i, be, nu: (be[i], 0, 0)),
                  pl.BlockSpec((1, ff, d), lambda i, be, nu: (be[i], 0, 0))],
        out_specs=pl.BlockSpec((blk, LANES), lambda i, be, nu: (i, 0)),
        scratch_shapes=[pltpu.VMEM((d, ff), BF16), pltpu.VMEM((d, ff), BF16), pltpu.VMEM((ff, d), BF16)],
    )
    return pl.pallas_call(
        _experts_kernel,
        grid_spec=grid_spec,
        out_shape=jax.ShapeDtypeStruct((rows * ROW_SUB, LANES), F32),
        compiler_params=_cparams("arbitrary"),
        name="moe_experts",
    )(blk_expert, n_used, xs, w1, w3, w2)


def _combine_kernel(final, dest_ref, dest_next_ref, x_ref, meta_ref, ys_hbm, g_ref, o_ref,
                    ya0_ref, ya1_ref, yb0_ref, yb1_ref, sem_a, sem_b):
    th = x_ref.shape[0] // 2
    i = pl.program_id(0)

    def gather(dref, off, bufs, sem):
        def row_copy(j, k):
            return pltpu.make_async_copy(ys_hbm.at[dref[off + 2 * j + k]], _row_of(bufs[k], j), sem)
        return row_copy

    copy_a = gather(dest_ref, 0, (ya0_ref, ya1_ref), sem_a)
    copy_b = gather(dest_ref, 2 * th, (yb0_ref, yb1_ref), sem_b)
    copy_a_next = gather(dest_next_ref, 0, (ya0_ref, ya1_ref), sem_a)

    def combine(half, y0_ref, y1_ref):
        rows = slice(half * th, (half + 1) * th)
        meta = meta_ref[rows, :]
        gate0 = meta[:, META_GATE:META_GATE + 1]
        gate1 = meta[:, META_GATE + 1:META_GATE + 2]
        cols = [slice(s * LANES, (s + 1) * LANES) for s in range(ROW_SUB)]
        ys = [x_ref[rows, cols[s]] + gate0 * _row_tile_load(y0_ref, th, s) + gate1 * _row_tile_load(y1_ref, th, s)
              for s in range(ROW_SUB)]
        if final:
            ssq = sum(jnp.sum(y * y, axis=-1, keepdims=True) for y in ys)
            inv = lax.rsqrt(ssq * (1.0 / D_MODEL) + EPS)
            ys = [y * inv * g_ref[:, cols[s]] for s, y in enumerate(ys)]
        for s, y in enumerate(ys):
            o_ref[rows, cols[s]] = y

    @pl.when(i == 0)
    def _():
        _start_rows(copy_a, th)
    _start_rows(copy_b, th)
    _wait_rows(copy_a, th)
    combine(0, ya0_ref, ya1_ref)

    @pl.when(i + 1 < pl.num_programs(0))
    def _():
        _start_rows(copy_a_next, th)
    _wait_rows(copy_b, th)
    combine(1, yb0_ref, yb1_ref)


def _combine(x2d, meta, dest, ys, final_g, final):
    n, d = x2d.shape
    tc = min(TC_COMBINE, n)
    nsteps = n // tc
    half_buf = pltpu.VMEM((tc // 2 * ROW_SUB, LANES), F32)
    return pl.pallas_call(
        functools.partial(_combine_kernel, final),
        grid=(nsteps,),
        in_specs=[pl.BlockSpec((2 * tc,), lambda i: (i,), memory_space=pltpu.SMEM),
                  pl.BlockSpec((2 * tc,), lambda i: (jnp.minimum(i + 1, nsteps - 1),), memory_space=pltpu.SMEM),
                  pl.BlockSpec((tc, d), lambda i: (i, 0)),
                  pl.BlockSpec((tc, LANES), lambda i: (i, 0)),
                  pl.BlockSpec(memory_space=pl.ANY),
                  pl.BlockSpec((1, d), lambda i: (0, 0))],
        out_specs=pl.BlockSpec((tc, d), lambda i: (i, 0)),
        out_shape=jax.ShapeDtypeStruct((n, d), F32),
        scratch_shapes=[half_buf, half_buf, half_buf, half_buf,
                        pltpu.SemaphoreType.DMA, pltpu.SemaphoreType.DMA],
        compiler_params=_cparams("arbitrary"),
        name="moe_combine_final" if final else "moe_combine",
    )(dest, dest, x2d, meta, ys, final_g.reshape(1, d))


def _moe(x2d, norm_g, w_group, w_expert, w1, w3, w2, layer, final_g, final):
    n, d = x2d.shape
    meta, cnt = _router(x2d, norm_g, w_group, w_expert)
    counts = cnt[EXP_LANE0:EXP_LANE0 + MOE_EXPERTS, 0].astype(jnp.int32)
    padded = (counts + BM_EXPERT - 1) // BM_EXPERT * BM_EXPERT
    pad_end = jnp.cumsum(padded)
    pad_start = pad_end - padded
    e = meta[:, META_E:META_E + 2].astype(jnp.int32)
    rank = meta[:, META_RANK:META_RANK + 2].astype(jnp.int32)
    eids = jnp.arange(MOE_EXPERTS, dtype=jnp.int32)
    dest = (jnp.sum(jnp.where(e[..., None] == eids, pad_start, 0), axis=-1) + rank).reshape(2 * n)
    rows = 2 * n + MOE_EXPERTS * BM_EXPERT
    nb = rows // BM_EXPERT
    blk_start = jnp.arange(nb, dtype=jnp.int32) * BM_EXPERT
    blk_expert = jnp.minimum(jnp.sum((pad_end[None, :] <= blk_start[:, None]).astype(jnp.int32), axis=1),
                             MOE_EXPERTS - 1)
    n_used = (pad_end[-1:] // BM_EXPERT).astype(jnp.int32)
    xs = _dispatch(x2d, norm_g, dest, pad_start + counts, pad_end, rows)
    ys = _experts(xs.reshape(rows * ROW_SUB, LANES), blk_expert + layer * MOE_EXPERTS, n_used, w1, w3, w2)
    return _combine(x2d, meta, dest, ys.reshape(rows, ROW_SUB, LANES), final_g, final)


def _split_w_in(w):
    n_ab = 2 * A_WIDTH + 4 * B_WIDTH
    n_c = n_ab + 2 * B_HEADS
    n_g = n_c + 2 * 3 * C_WIDTH + C_WIDTH
    w_main = jnp.concatenate([w[:, n_g:], w[:, :n_ab]], axis=1).astype(BF16)
    w_c = w[:, n_c:n_g].astype(BF16)
    w_small = jnp.pad(w[:, n_ab:n_c], ((0, 0), (0, LANES - 2 * B_HEADS))).astype(BF16)
    return w_main, w_c, w_small


def kernel(x, norm_mix_g, w_in, a_ln_g, a_ln_b, a_w_s, a_b_s, b_conv_w, b_a_log, b_dt_bias, b_norm_g, rel_bias,
           w_branch, w_out, norm_ffn_g, w_group, w_expert, w1, w3, w2, final_norm_g):
    batch, seq, d = x.shape
    depth = w_in.shape[0]
    x2d = x.reshape(batch * seq, d)
    for l in range(depth):
        w_main, w_c, w_small = _split_w_in(w_in[l])
        assert w_main.shape[1] == PROJ_W and w_c.shape[1] == CPROJ_W
        proj, cproj, p2 = _in_proj(x2d, norm_mix_g[l], w_main, w_c, w_small)
        yb = _deltanet(proj, p2, b_conv_w[l], b_a_log[l], b_dt_bias[l], b_norm_g[l], batch, seq)
        yc = _dilated_attn(cproj, rel_bias, batch, seq)
        x2d = _merge(x2d, proj, a_w_s[l], a_b_s[l], a_ln_g[l], a_ln_b[l], yb, yc, w_branch[l], w_out[l])
        x2d = _moe(x2d, norm_ffn_g[l], w_group[l], w_expert[l], w1.reshape((-1,) + w1.shape[2:]),
                   w3.reshape((-1,) + w3.shape[2:]), w2.reshape((-1,) + w2.shape[2:]), l, final_norm_g,
                   final=(l == depth - 1))
    return x2d.reshape(batch, seq, d)
```
